```python
import math
import jax, jax.numpy as jnp
from jax import lax
import numpy as np

D_MODEL = 1024
BATCH = 4
SEQ = 4096
DEPTH = 2

ATT_HEADS = 8
ATT_KV_HEADS = 2
ATT_HEAD_DIM = 64
ATT_WIDTH = ATT_HEADS * ATT_HEAD_DIM
ATT_KV_WIDTH = ATT_KV_HEADS * ATT_HEAD_DIM
WINDOW = 128
CONV_WIDTH = 512
CONV_K = 31
DN_HEADS = 4
DN_HEAD_DIM = 128
DN_WIDTH = DN_HEADS * DN_HEAD_DIM
DN_CONV_K = 4
DN_CHUNK = 64
N_BRANCH = 3
EPS = 1e-6
NEG_INF = -1e30
IN_SIZES = (ATT_WIDTH, ATT_KV_WIDTH, ATT_KV_WIDTH, ATT_WIDTH,
            2 * CONV_WIDTH, CONV_WIDTH,
            DN_WIDTH, DN_WIDTH, DN_WIDTH, DN_HEADS, DN_HEADS, DN_WIDTH,
            N_BRANCH * D_MODEL)
D_IN = sum(IN_SIZES)

kernel_name = "hybrid_swa_conformer_gdn_gated_merge"


def rms_norm(x, g):
    xf = x.astype(jnp.float32)
    y = xf * lax.rsqrt(jnp.mean(xf * xf, axis=-1, keepdims=True) + EPS)
    return (y * g.astype(jnp.float32)).astype(x.dtype)


def layer_norm(x, g, b):
    xf = x.astype(jnp.float32)
    mu = jnp.mean(xf, axis=-1, keepdims=True)
    xc = xf - mu
    var = jnp.mean(xc * xc, axis=-1, keepdims=True)
    y = xc * lax.rsqrt(var + EPS) * g.astype(jnp.float32) + b.astype(jnp.float32)
    return y.astype(x.dtype)


def l2_norm(x):
    return x * lax.rsqrt(jnp.sum(x * x, axis=-1, keepdims=True) + EPS)


def causal_dwconv(x, w):
    k_width, ch = w.shape
    return lax.conv_general_dilated(
        x, w[:, None, :].astype(x.dtype), window_strides=(1,), padding=[(k_width - 1, 0)],
        dimension_numbers=("NWC", "WIO", "NWC"), feature_group_count=ch)


def alibi_slopes(n_heads):
    return jnp.exp2(-8.0 * jnp.arange(1, n_heads + 1, dtype=jnp.float32) / n_heads)


def sliding_window_attention(q, k, v, sinks):
    bsz, seq, n_h, d = q.shape
    n_kv = k.shape[2]
    grp = n_h // n_kv
    nb = seq // WINDOW
    qb = q.reshape(bsz, nb, WINDOW, n_kv, grp, d)

    def with_prev(t):
        tb = t.reshape(bsz, nb, WINDOW, n_kv, d)
        prev = jnp.concatenate([jnp.zeros_like(tb[:, :1]), tb[:, :-1]], axis=1)
        return jnp.concatenate([prev, tb], axis=2)

    kc, vc = with_prev(k), with_prev(v)
    s = jnp.einsum("bnqhgd,bnkhd->bhgnqk", qb, kc,
                   preferred_element_type=jnp.float32)
    qi = jnp.arange(WINDOW)[:, None]
    kj = jnp.arange(2 * WINDOW)[None, :]
    dist = qi + WINDOW - kj
    blk = jnp.arange(nb)[:, None, None]
    valid = (dist >= 0) & (dist < WINDOW) & ((blk > 0) | (kj >= WINDOW))
    slopes = alibi_slopes(n_h).reshape(n_kv, grp)[:, :, None, None, None]
    s = s - slopes * dist.astype(jnp.float32)
    s = jnp.where(valid, s, NEG_INF)
    sink = sinks.astype(jnp.float32).reshape(n_kv, grp)[:, :, None, None, None]
    m = jnp.maximum(jnp.max(s, axis=-1, keepdims=True), sink)
    p = jnp.exp(s - m)
    denom = jnp.sum(p, axis=-1, keepdims=True) + jnp.exp(sink - m)
    p = (p / denom).astype(v.dtype)
    o = jnp.einsum("bhgnqk,bnkhd->bnqhgd", p, vc)
    return o.reshape(bsz, seq, n_h * d)


def gated_delta_rule(q, k, v, g, beta):
    bsz, seq, n_h, dk = q.shape
    dv = v.shape[-1]
    cs = DN_CHUNK
    nc = seq // cs

    def chunks(t):
        return jnp.moveaxis(t.reshape(bsz, nc, cs, n_h, *t.shape[3:]), 3, 1)

    q, k, v, g, beta = chunks(q), chunks(k), chunks(v), chunks(g), chunks(beta)
    gc = jnp.cumsum(g, axis=-1)
    kb = k * beta[..., None]
    vb = v * beta[..., None]
    idx = jnp.arange(cs)
    lower = idx[:, None] >= idx[None, :]
    strict = idx[:, None] > idx[None, :]
    diff = gc[..., :, None] - gc[..., None, :]
    decay = jnp.where(lower, jnp.exp(jnp.where(lower, diff, 0.0)), 0.0)
    a = jnp.where(strict, jnp.einsum("bhncd,bhnsd->bhncs", kb, k) * decay, 0.0)
    eye = jnp.eye(cs, dtype=a.dtype)
    tmat = lax.linalg.triangular_solve(eye + a, jnp.broadcast_to(eye, a.shape),
                                       left_side=True, lower=True)
    u = tmat @ vb
    w = tmat @ (kb * jnp.exp(gc)[..., None])
    intra = jnp.where(lower, jnp.einsum("bhncd,bhnsd->bhncs", q, k) * decay, 0.0)
    qe = q * jnp.exp(gc)[..., None]
    g_last = gc[..., -1]
    ke = k * jnp.exp(g_last[..., None] - gc)[..., None]

    def step(state, inp):
        u_c, w_c, qe_c, ke_c, intra_c, gl = inp
        v_new = u_c - w_c @ state
        o_c = qe_c @ state + intra_c @ v_new
        state = state * jnp.exp(gl)[..., None, None] + jnp.swapaxes(ke_c, -1, -2) @ v_new
        return state, o_c

    xs = (jnp.moveaxis(u, 2, 0), jnp.moveaxis(w, 2, 0), jnp.moveaxis(qe, 2, 0),
          jnp.moveaxis(ke, 2, 0), jnp.moveaxis(intra, 2, 0), jnp.moveaxis(g_last, 2, 0))
    s0 = jnp.zeros((bsz, n_h, dk, dv), jnp.float32)
    _, o = lax.scan(step, s0, xs)
    return jnp.transpose(o, (1, 0, 3, 2, 4)).reshape(bsz, seq, n_h, dv)


def hybrid_layer(x, c, w_ada, b_ada, norm_g, w_in, q_norm_g, k_norm_g, sinks,
                 dw_w, dw_b, ln_g, ln_b, pw2_w, pw2_b, sconv_w, a_log, dt_bias, dn_norm_g,
                 w_proj_a, w_proj_b, w_proj_c, w_out):
    bsz, seq, _ = x.shape
    mod = jax.nn.silu(c) @ w_ada + b_ada
    shift, scale, gate = jnp.split(mod, 3, axis=-1)
    h = rms_norm(x, norm_g) * (1.0 + scale[:, None, :]) + shift[:, None, :]

    proj = h @ w_in
    split_points = np.cumsum(IN_SIZES)[:-1].tolist()
    (qa, ka, va, za, glu_in, zb, qc, kc, vc, ac, bc, zc, mg) = jnp.split(proj, split_points, axis=-1)

    qa = rms_norm(qa.reshape(bsz, seq, ATT_HEADS, ATT_HEAD_DIM), q_norm_g) * (ATT_HEAD_DIM ** -0.5)
    ka = rms_norm(ka.reshape(bsz, seq, ATT_KV_HEADS, ATT_HEAD_DIM), k_norm_g)
    va = va.reshape(bsz, seq, ATT_KV_HEADS, ATT_HEAD_DIM)
    ya = sliding_window_attention(qa, ka, va, sinks) * jax.nn.silu(za)

    val, gt = jnp.split(glu_in, 2, axis=-1)
    ub = val * jax.nn.sigmoid(gt)
    ub = causal_dwconv(ub, dw_w) + dw_b
    ub = jax.nn.silu(layer_norm(ub, ln_g, ln_b))
    yb = (ub @ pw2_w + pw2_b) * jax.nn.silu(zb)

    qkv = jax.nn.silu(causal_dwconv(jnp.concatenate([qc, kc, vc], axis=-1), sconv_w))
    qc, kc, vc = jnp.split(qkv, 3, axis=-1)
    qd = l2_norm(qc.reshape(bsz, seq, DN_HEADS, DN_HEAD_DIM).astype(jnp.float32)) * (DN_HEAD_DIM ** -0.5)
    kd = l2_norm(kc.reshape(bsz, seq, DN_HEADS, DN_HEAD_DIM).astype(jnp.float32))
    vd = vc.reshape(bsz, seq, DN_HEADS, DN_HEAD_DIM).astype(jnp.float32)
    beta = jax.nn.sigmoid(bc.astype(jnp.float32))
    g = -jnp.exp(a_log.astype(jnp.float32)) * jax.nn.softplus(
        ac.astype(jnp.float32) + dt_bias.astype(jnp.float32))
    od = gated_delta_rule(qd, kd, vd, g, beta)
    od = rms_norm(od, dn_norm_g).astype(x.dtype).reshape(bsz, seq, DN_WIDTH)
    yc = od * jax.nn.silu(zc)

    ga, gb, gcb = jnp.split(jax.nn.sigmoid(mg), N_BRANCH, axis=-1)
    merged = ga * (ya @ w_proj_a) + gb * (yb @ w_proj_b) + gcb * (yc @ w_proj_c)
    return x + gate[:, None, :] * (merged @ w_out)


def setup_inputs(seed: int = 0) -> dict:
    key = jax.random.key(seed)
    ks = jax.random.split(key, 24)
    L, D = DEPTH, D_MODEL
    f32 = jnp.float32

    def nrm(k, shape, scale):
        return jax.random.normal(k, shape, f32) * scale

    dt = jnp.exp(jax.random.uniform(ks[17], (L, DN_HEADS), f32,
                                    minval=math.log(1e-3), maxval=math.log(1e-1)))
    return {
        "x": nrm(ks[0], (BATCH, SEQ, D), 1.0),
        "c": nrm(ks[1], (BATCH, D), 1.0),
        "w_ada": nrm(ks[2], (L, D, 3 * D), 0.5 * D ** -0.5),
        "b_ada": nrm(ks[3], (L, 3 * D), 0.02),
        "norm_g": 1.0 + nrm(ks[4], (L, D), 0.02),
        "w_in": nrm(ks[5], (L, D, D_IN), D ** -0.5),
        "q_norm_g": 1.0 + nrm(ks[6], (L, ATT_HEAD_DIM), 0.02),
        "k_norm_g": 1.0 + nrm(ks[7], (L, ATT_HEAD_DIM), 0.02),
        "sinks": nrm(ks[8], (L, ATT_HEADS), 0.5),
        "dw_w": nrm(ks[9], (L, CONV_K, CONV_WIDTH), CONV_K ** -0.5),
        "dw_b": nrm(ks[10], (L, CONV_WIDTH), 0.02),
        "ln_g": 1.0 + nrm(ks[11], (L, CONV_WIDTH), 0.02),
        "ln_b": nrm(ks[12], (L, CONV_WIDTH), 0.02),
        "pw2_w": nrm(ks[13], (L, CONV_WIDTH, CONV_WIDTH), CONV_WIDTH ** -0.5),
        "pw2_b": nrm(ks[14], (L, CONV_WIDTH), 0.02),
        "sconv_w": nrm(ks[15], (L, DN_CONV_K, 3 * DN_WIDTH), DN_CONV_K ** -0.5),
        "a_log": jnp.log(jax.random.uniform(ks[16], (L, DN_HEADS), f32, minval=1.0, maxval=16.0)),
        "dt_bias": dt + jnp.log(-jnp.expm1(-dt)),
        "dn_norm_g": 1.0 + nrm(ks[18], (L, DN_HEAD_DIM), 0.02),
        "w_proj_a": nrm(ks[19], (L, ATT_WIDTH, D), ATT_WIDTH ** -0.5),
        "w_proj_b": nrm(ks[20], (L, CONV_WIDTH, D), CONV_WIDTH ** -0.5),
        "w_proj_c": nrm(ks[21], (L, DN_WIDTH, D), DN_WIDTH ** -0.5),
        "w_out": nrm(ks[22], (L, D, D), D ** -0.5),
    }


def reference(x, c, w_ada, b_ada, norm_g, w_in, q_norm_g, k_norm_g, sinks,
              dw_w, dw_b, ln_g, ln_b, pw2_w, pw2_b, sconv_w, a_log, dt_bias, dn_norm_g,
              w_proj_a, w_proj_b, w_proj_c, w_out):
    for l in range(DEPTH):
        x = hybrid_layer(x, c, w_ada[l], b_ada[l], norm_g[l], w_in[l], q_norm_g[l], k_norm_g[l],
                         sinks[l], dw_w[l], dw_b[l], ln_g[l], ln_b[l], pw2_w[l], pw2_b[l],
                         sconv_w[l], a_log[l], dt_bias[l], dn_norm_g[l],
                         w_proj_a[l], w_proj_b[l], w_proj_c[l], w_out[l])
    return x
```

```python
import functools

import numpy as np
import jax
import jax.numpy as jnp
from jax import lax
from jax.experimental import pallas as pl
from jax.experimental.pallas import tpu as pltpu

F32 = jnp.float32
BF16 = jnp.bfloat16

D_MODEL = 1024
ATT_HEADS = 8
ATT_KV_HEADS = 2
ATT_HEAD_DIM = 64
ATT_GROUP = ATT_HEADS // ATT_KV_HEADS
ATT_WIDTH = ATT_HEADS * ATT_HEAD_DIM
ATT_KV_WIDTH = ATT_KV_HEADS * ATT_HEAD_DIM
GROUP_WIDTH = ATT_GROUP * ATT_HEAD_DIM
WINDOW = 128
CONV_WIDTH = 512
CONV_K = 31
DN_HEADS = 4
DN_HEAD_DIM = 128
DN_WIDTH = DN_HEADS * DN_HEAD_DIM
DN_CONV_K = 4
DN_CHUNK = 64
EPS = 1e-6
NEG_INF = -1e30

LANES = 128
SUBLANES = 8

TILE = 256
CONV_PAD = 32
SCONV_PAD = 8
CONV_ROWS = 32
VMEM_LIMIT = 56 * 1024 * 1024

OFF_A = 0
OFF_B = OFF_A + 2 * ATT_WIDTH + 2 * ATT_KV_WIDTH
OFF_C = OFF_B + 3 * CONV_WIDTH
OFF_ZC = OFF_C + 3 * DN_WIDTH
OFF_MG = OFF_ZC + DN_WIDTH
OFF_AB = OFF_MG + 3 * D_MODEL
W_COLS = OFF_AB + LANES


def _dot(a, b, precision=None):
    return jnp.dot(a, b, preferred_element_type=F32, precision=precision)


def _dot_nt(a, b):
    return lax.dot_general(a, b, (((1,), (1,)), ((), ())), preferred_element_type=F32)


def _silu(x):
    return x * jax.nn.sigmoid(x)


def _softplus(x):
    return jnp.maximum(x, 0.0) + jnp.log1p(jnp.exp(-jnp.abs(x)))


def _ada_body(c_ref, w_ref, b_ref, o_ref):
    sc = _silu(c_ref[...]).astype(BF16)
    o_ref[...] = _dot(sc, w_ref[...].astype(BF16)) + b_ref[...]


def _ada_call(c_pad, w_ada, b_ada):
    depth, d, d3 = w_ada.shape
    nblk = d3 // d
    return pl.pallas_call(
        _ada_body,
        grid=(depth, nblk),
        in_specs=[
            pl.BlockSpec((SUBLANES, d), lambda l, j: (0, 0)),
            pl.BlockSpec((None, d, d), lambda l, j: (l, 0, j)),
            pl.BlockSpec((None, 1, d), lambda l, j: (l, 0, j)),
        ],
        out_specs=pl.BlockSpec((None, SUBLANES, d), lambda l, j: (l, 0, j)),
        out_shape=jax.ShapeDtypeStruct((depth, SUBLANES, d3), F32),
        name="adaln_mod",
    )(c_pad, w_ada, b_ada.reshape(depth, 1, d3))


def _layer_body(x_ref, shift_ref, scale_ref, gate_ref, ng_ref, w_ref, wabt_ref, qg_ref, kg_ref,
                sink_ref, bias_ref, dww_ref, dwb_ref, lng_ref, lnb_ref, pw2w_ref, pw2b_ref,
                scw_ref, alog_r_ref, dtb_r_ref, alog_c_ref, dtb_c_ref, dng_ref,
                wpa_ref, wpb_ref, wpc_ref, wout_ref,
                o_ref,
                kprev_ref, vprev_ref, convbuf_ref, sconvbuf_ref, state_ref,
                ya_ref, cv_ref, qkv_ref, yc_ref):
    tm = x_ref.shape[0]
    t = pl.program_id(1)

    @pl.when(t == 0)
    def _reset():
        kprev_ref[...] = jnp.zeros_like(kprev_ref)
        vprev_ref[...] = jnp.zeros_like(vprev_ref)
        convbuf_ref[0:CONV_PAD, :] = jnp.zeros((CONV_PAD, CONV_WIDTH), F32)
        sconvbuf_ref[0:SCONV_PAD, :] = jnp.zeros((SCONV_PAD, 3 * DN_WIDTH), F32)
        state_ref[...] = jnp.zeros_like(state_ref)

    x = x_ref[...]
    ms = jnp.mean(x * x, axis=-1, keepdims=True)
    h = x * lax.rsqrt(ms + EPS) * ng_ref[...]
    h = h * (1.0 + scale_ref[...]) + shift_ref[...]
    hb = h.astype(BF16)

    def proj(lo, hi):
        return _dot(hb, w_ref[:, lo:hi])

    pa = proj(OFF_A, OFF_B)
    qa = pa[:, 0:ATT_WIDTH]
    ka = pa[:, ATT_WIDTH:ATT_WIDTH + ATT_KV_WIDTH]
    va = pa[:, ATT_WIDTH + ATT_KV_WIDTH:ATT_WIDTH + 2 * ATT_KV_WIDTH]
    za = pa[:, ATT_WIDTH + 2 * ATT_KV_WIDTH:]

    def head_mean_square(v, width):
        r = lax.broadcasted_iota(jnp.int32, (width, width), 0) // ATT_HEAD_DIM
        c = lax.broadcasted_iota(jnp.int32, (width, width), 1) // ATT_HEAD_DIM
        avg = jnp.where(r == c, 1.0 / ATT_HEAD_DIM, 0.0).astype(BF16)
        sq = v * v
        hi = sq.astype(BF16)
        lo = (sq - hi.astype(F32)).astype(BF16)
        return _dot(hi, avg) + _dot(lo, avg)

    qn = qa * lax.rsqrt(head_mean_square(qa, ATT_WIDTH) + EPS) * qg_ref[...] * (ATT_HEAD_DIM ** -0.5)
    kn = ka * lax.rsqrt(head_mean_square(ka, ATT_KV_WIDTH) + EPS) * kg_ref[...]

    low_half = lax.broadcasted_iota(jnp.int32, (tm, ATT_KV_WIDTH), 1) < ATT_HEAD_DIM

    def replicate(v):
        swapped = pltpu.roll(v, ATT_HEAD_DIM, axis=1)
        head0 = jnp.where(low_half, v, swapped)
        head1 = jnp.where(low_half, swapped, v)
        return [jnp.concatenate([hd, hd], axis=1).astype(BF16) for hd in (head0, head1)]

    krep = replicate(kn)
    vrep = replicate(va)

    q_head = lax.broadcasted_iota(jnp.int32, (WINDOW, GROUP_WIDTH), 1) // ATT_HEAD_DIM
    v_head = lax.broadcasted_iota(jnp.int32, (2 * WINDOW, GROUP_WIDTH), 1) // ATT_HEAD_DIM
    key_col = lax.broadcasted_iota(jnp.int32, (ATT_GROUP * WINDOW, 2 * WINDOW), 1)
    n_masked_cols = jnp.where(t == 0, WINDOW, 0)

    for j in range(tm // WINDOW):
        rows = slice(j * WINDOW, (j + 1) * WINDOW)
        for g in range(ATT_KV_HEADS):
            if j == 0:
                k_prev, v_prev = kprev_ref[g], vprev_ref[g]
            else:
                prev_rows = slice((j - 1) * WINDOW, j * WINDOW)
                k_prev, v_prev = krep[g][prev_rows], vrep[g][prev_rows]
            kcat = jnp.concatenate([k_prev, krep[g][rows]], axis=0)
            vcat = jnp.concatenate([v_prev, vrep[g][rows]], axis=0)
            qg = qn[rows, g * GROUP_WIDTH:(g + 1) * GROUP_WIDTH]
            qstack = jnp.concatenate(
                [jnp.where(q_head == i, qg, 0.0) for i in range(ATT_GROUP)], axis=0).astype(BF16)
            s = _dot_nt(qstack, kcat) + bias_ref[g]
            if j == 0:
                s = jnp.where(key_col < n_masked_cols, NEG_INF, s)
            sink = sink_ref[g]
            m = jnp.maximum(jnp.max(s, axis=-1, keepdims=True), sink)
            p = jnp.exp(s - m)
            denom = jnp.sum(p, axis=-1, keepdims=True) + jnp.exp(sink - m)
            p = (p / denom).astype(BF16)
            o = None
            for i in range(ATT_GROUP):
                v_i = jnp.where(v_head == i, vcat, jnp.zeros_like(vcat))
                o_i = _dot(p[i * WINDOW:(i + 1) * WINDOW], v_i)
                o = o_i if o is None else o + o_i
            ya_ref[rows, g * GROUP_WIDTH:(g + 1) * GROUP_WIDTH] = o

    for g in range(ATT_KV_HEADS):
        kprev_ref[g] = krep[g][tm - WINDOW:tm]
        vprev_ref[g] = vrep[g][tm - WINDOW:tm]

    ya = ya_ref[...] * _silu(za)
    merged = _dot(ya.astype(BF16), wpa_ref[...]) * jax.nn.sigmoid(proj(OFF_MG, OFF_MG + D_MODEL))

    pb = proj(OFF_B, OFF_C)
    ub = pb[:, 0:CONV_WIDTH] * jax.nn.sigmoid(pb[:, CONV_WIDTH:2 * CONV_WIDTH])
    zb = pb[:, 2 * CONV_WIDTH:]
    convbuf_ref[CONV_PAD:CONV_PAD + tm, :] = ub
    base = CONV_PAD - (CONV_K - 1)
    for r in range(tm // CONV_ROWS):
        acc = jnp.broadcast_to(dwb_ref[...], (CONV_ROWS, CONV_WIDTH))
        for k in range(CONV_K):
            start = base + k + r * CONV_ROWS
            acc = acc + dww_ref[k:k + 1, :] * convbuf_ref[start:start + CONV_ROWS, :]
        cv_ref[r * CONV_ROWS:(r + 1) * CONV_ROWS, :] = acc
    convbuf_ref[0:CONV_PAD, :] = convbuf_ref[tm:tm + CONV_PAD, :]
    cv = cv_ref[...]
    mu = jnp.mean(cv, axis=-1, keepdims=True)
    xc = cv - mu
    var = jnp.mean(xc * xc, axis=-1, keepdims=True)
    ln = xc * lax.rsqrt(var + EPS) * lng_ref[...] + lnb_ref[...]
    yb = (_dot(_silu(ln).astype(BF16), pw2w_ref[...]) + pw2b_ref[...]) * _silu(zb)
    merged = merged + _dot(yb.astype(BF16), wpb_ref[...]) * jax.nn.sigmoid(
        proj(OFF_MG + D_MODEL, OFF_MG + 2 * D_MODEL))

    sconvbuf_ref[SCONV_PAD:SCONV_PAD + tm, :] = proj(OFF_C, OFF_ZC)
    sbase = SCONV_PAD - (DN_CONV_K - 1)
    for r in range(tm // CONV_ROWS):
        for cblk in range(3):
            cols = slice(cblk * DN_WIDTH, (cblk + 1) * DN_WIDTH)
            acc = None
            for k in range(DN_CONV_K):
                start = sbase + k + r * CONV_ROWS
                term = scw_ref[k:k + 1, cols] * sconvbuf_ref[start:start + CONV_ROWS, cols]
                acc = term if acc is None else acc + term
            qkv_ref[r * CONV_ROWS:(r + 1) * CONV_ROWS, cols] = _silu(acc)
    sconvbuf_ref[0:SCONV_PAD, :] = sconvbuf_ref[tm:tm + SCONV_PAD, :]

    ab = proj(OFF_AB, W_COLS)
    ab_t = _dot_nt(wabt_ref[...], hb)
    g_col = -jnp.exp(alog_r_ref[...]) * _softplus(ab + dtb_r_ref[...])
    g_row = -jnp.exp(alog_c_ref[...]) * _softplus(ab_t + dtb_c_ref[...])
    beta_col = jax.nn.sigmoid(ab)

    ri = lax.broadcasted_iota(jnp.int32, (tm, tm), 0)
    ci = lax.broadcasted_iota(jnp.int32, (tm, tm), 1)
    same_chunk = (ri // DN_CHUNK) == (ci // DN_CHUNK)
    lower = same_chunk & (ri >= ci)
    strict = same_chunk & (ri > ci)
    upper = same_chunk & (ri <= ci)
    hp = lax.Precision.HIGHEST
    gc_col = _dot(jnp.where(lower, 1.0, 0.0), g_col, precision=hp)
    gl_col = _dot(jnp.where(same_chunk, 1.0, 0.0), g_col, precision=hp)
    gc_row = _dot(g_row, jnp.where(upper, 1.0, 0.0), precision=hp)

    for hd in range(DN_HEADS):
        hcols = slice(hd * DN_HEAD_DIM, (hd + 1) * DN_HEAD_DIM)
        qh = qkv_ref[:, hd * DN_HEAD_DIM:(hd + 1) * DN_HEAD_DIM]
        kh = qkv_ref[:, DN_WIDTH + hd * DN_HEAD_DIM:DN_WIDTH + (hd + 1) * DN_HEAD_DIM]
        vh = qkv_ref[:, 2 * DN_WIDTH + hd * DN_HEAD_DIM:2 * DN_WIDTH + (hd + 1) * DN_HEAD_DIM]
        qh = qh * lax.rsqrt(jnp.sum(qh * qh, axis=-1, keepdims=True) + EPS) * (DN_HEAD_DIM ** -0.5)
        kh = kh * lax.rsqrt(jnp.sum(kh * kh, axis=-1, keepdims=True) + EPS)
        beta = beta_col[:, DN_HEADS + hd:DN_HEADS + hd + 1]
        gc = gc_col[:, hd:hd + 1]
        gl = gl_col[:, hd:hd + 1]
        gr = gc_row[hd:hd + 1, :]
        kb = kh * beta
        vb = vh * beta
        eg = jnp.exp(gc)
        qe = qh * eg
        ke = kh * jnp.exp(gl - gc)
        decay = jnp.exp(jnp.where(lower, gc - gr, NEG_INF))
        khb = kh.astype(BF16)
        a_mat = jnp.where(strict, _dot_nt(kb.astype(BF16), khb) * decay, 0.0)
        intra = _dot_nt(qh.astype(BF16), khb) * decay
        pw = a_mat
        q_mat = -a_mat
        n_sq = int(np.log2(DN_CHUNK)) - 1
        for _ in range(n_sq):
            pwb = pw.astype(BF16)
            pw = _dot(pwb, pwb)
            q_mat = q_mat + pw + _dot(q_mat.astype(BF16), pw.astype(BF16))
        xin = jnp.concatenate([vb, kb * eg], axis=1)
        uw = xin + _dot(q_mat.astype(BF16), xin.astype(BF16))
        u = uw[:, 0:DN_HEAD_DIM]
        w = uw[:, DN_HEAD_DIM:]

        state = state_ref[hd]
        v_news, o_inter = [], []
        for c in range(tm // DN_CHUNK):
            crow = slice(c * DN_CHUNK, (c + 1) * DN_CHUNK)
            wq = jnp.concatenate([w[crow], qe[crow]], axis=0).astype(BF16)
            res = _dot(wq, state.astype(BF16))
            v_new = u[crow] - res[0:DN_CHUNK]
            o_inter.append(res[DN_CHUNK:])
            g_last = gl[c * DN_CHUNK:c * DN_CHUNK + 1, :]
            state = state * jnp.exp(g_last) + _dot(ke[crow].T.astype(BF16), v_new.astype(BF16))
            v_news.append(v_new)
        state_ref[hd] = state
        v_new_all = jnp.concatenate(v_news, axis=0)
        od = jnp.concatenate(o_inter, axis=0) + _dot(intra.astype(BF16), v_new_all.astype(BF16))
        od = od * lax.rsqrt(jnp.mean(od * od, axis=-1, keepdims=True) + EPS) * dng_ref[...]
        yc_ref[:, hcols] = od

    yc = yc_ref[...] * _silu(proj(OFF_ZC, OFF_MG))
    merged = merged + _dot(yc.astype(BF16), wpc_ref[...]) * jax.nn.sigmoid(
        proj(OFF_MG + 2 * D_MODEL, OFF_AB))

    o_ref[...] = x + gate_ref[...] * _dot(merged.astype(BF16), wout_ref[...])


def _const_spec(shape):
    nd = len(shape)
    return pl.BlockSpec(shape, lambda b, t: (0,) * nd, pipeline_mode=pl.Buffered(1))


def _alibi_bias():
    qi = np.arange(WINDOW)[:, None]
    kj = np.arange(2 * WINDOW)[None, :]
    dist = qi + WINDOW - kj
    valid = (dist >= 0) & (dist < WINDOW)
    slopes = np.exp2(-8.0 * np.arange(1, ATT_HEADS + 1, dtype=np.float32) / ATT_HEADS)
    bias = np.where(valid[None], -slopes[:, None, None] * dist[None].astype(np.float32), NEG_INF)
    return jnp.asarray(bias.reshape(ATT_KV_HEADS, ATT_GROUP * WINDOW, 2 * WINDOW), dtype=F32)


def _layer_call(x, shift, scale, gate, p):
    bsz, seq, d = x.shape
    tm = TILE
    assert seq % tm == 0 and d == D_MODEL
    consts = [p["norm_g"], p["w"], p["wab_t"], p["q_norm_g"], p["k_norm_g"], p["sinks"], p["bias"],
              p["dw_w"], p["dw_b"], p["ln_g"], p["ln_b"], p["pw2_w"], p["pw2_b"], p["sconv_w"],
              p["alog_r"], p["dtb_r"], p["alog_c"], p["dtb_c"], p["dn_norm_g"],
              p["w_proj_a"], p["w_proj_b"], p["w_proj_c"], p["w_out"]]
    tile_spec = pl.BlockSpec((None, tm, d), lambda b, t: (b, t, 0))
    mod_spec = pl.BlockSpec((None, 1, d), lambda b, t: (b, 0, 0))
    return pl.pallas_call(
        _layer_body,
        grid=(bsz, seq // tm),
        in_specs=[tile_spec, mod_spec, mod_spec, mod_spec] + [_const_spec(a.shape) for a in consts],
        out_specs=tile_spec,
        out_shape=jax.ShapeDtypeStruct(x.shape, x.dtype),
        scratch_shapes=[
            pltpu.VMEM((ATT_KV_HEADS, WINDOW, GROUP_WIDTH), BF16),
            pltpu.VMEM((ATT_KV_HEADS, WINDOW, GROUP_WIDTH), BF16),
            pltpu.VMEM((CONV_PAD + tm, CONV_WIDTH), F32),
            pltpu.VMEM((SCONV_PAD + tm, 3 * DN_WIDTH), F32),
            pltpu.VMEM((DN_HEADS, DN_HEAD_DIM, DN_HEAD_DIM), F32),
            pltpu.VMEM((tm, ATT_WIDTH), F32),
            pltpu.VMEM((tm, CONV_WIDTH), F32),
            pltpu.VMEM((tm, 3 * DN_WIDTH), F32),
            pltpu.VMEM((tm, DN_WIDTH), F32),
        ],
        compiler_params=pltpu.CompilerParams(
            dimension_semantics=("arbitrary", "arbitrary"),
            vmem_limit_bytes=VMEM_LIMIT),
        name="hybrid_layer",
    )(x, shift, scale, gate, *consts)


def _prep_layer(l, w_in, norm_g, q_norm_g, k_norm_g, sinks, dw_w, dw_b, ln_g, ln_b, pw2_w, pw2_b,
                sconv_w, a_log, dt_bias, dn_norm_g, w_proj_a, w_proj_b, w_proj_c, w_out):
    d = w_in.shape[1]
    wl = w_in[l]
    ab_lo = OFF_ZC
    ab_hi = ab_lo + 2 * DN_HEADS
    w_ab = wl[:, ab_lo:ab_hi]
    w = jnp.concatenate(
        [wl[:, :ab_lo], wl[:, ab_hi:], w_ab, jnp.zeros((d, LANES - 2 * DN_HEADS), wl.dtype)],
        axis=1).astype(BF16)

    def lane_pad(v):
        return jnp.zeros((1, LANES), F32).at[0, :DN_HEADS].set(v)

    def sub_pad(v):
        return jnp.zeros((SUBLANES, 1), F32).at[:DN_HEADS, 0].set(v)

    return dict(
        norm_g=norm_g[l][None, :],
        w=w,
        wab_t=w_ab.T.astype(BF16),
        q_norm_g=jnp.tile(q_norm_g[l], ATT_HEADS)[None, :],
        k_norm_g=jnp.tile(k_norm_g[l], ATT_KV_HEADS)[None, :],
        sinks=jnp.repeat(sinks[l], WINDOW).reshape(ATT_KV_HEADS, ATT_GROUP * WINDOW, 1),
        bias=_alibi_bias(),
        dw_w=jnp.concatenate([dw_w[l], jnp.zeros((1, CONV_WIDTH), F32)], axis=0),
        dw_b=dw_b[l][None, :],
        ln_g=ln_g[l][None, :],
        ln_b=ln_b[l][None, :],
        pw2_w=pw2_w[l].astype(BF16),
        pw2_b=pw2_b[l][None, :],
        sconv_w=jnp.concatenate([sconv_w[l], jnp.zeros((SUBLANES - DN_CONV_K, 3 * DN_WIDTH), F32)], axis=0),
        alog_r=lane_pad(a_log[l]),
        dtb_r=lane_pad(dt_bias[l]),
        alog_c=sub_pad(a_log[l]),
        dtb_c=sub_pad(dt_bias[l]),
        dn_norm_g=dn_norm_g[l][None, :],
        w_proj_a=w_proj_a[l].astype(BF16),
        w_proj_b=w_proj_b[l].astype(BF16),
        w_proj_c=w_proj_c[l].astype(BF16),
        w_out=w_out[l].astype(BF16),
    )


@jax.jit
def _forward(x, c, w_ada, b_ada, norm_g, w_in, q_norm_g, k_norm_g, sinks, dw_w, dw_b, ln_g, ln_b,
             pw2_w, pw2_b, sconv_w, a_log, dt_bias, dn_norm_g, w_proj_a, w_proj_b, w_proj_c, w_out):
    bsz, _, d = x.shape
    depth = w_ada.shape[0]
    c_pad = jnp.zeros((SUBLANES, d), F32).at[:bsz].set(c)
    mod = _ada_call(c_pad, w_ada, b_ada)
    for l in range(depth):
        p = _prep_layer(l, w_in, norm_g, q_norm_g, k_norm_g, sinks, dw_w, dw_b, ln_g, ln_b, pw2_w,
                        pw2_b, sconv_w, a_log, dt_bias, dn_norm_g, w_proj_a, w_proj_b, w_proj_c, w_out)
        ml = mod[l, :bsz]
        shift = ml[:, None, 0:d]
        scale = ml[:, None, d:2 * d]
        gate = ml[:, None, 2 * d:3 * d]
        x = _layer_call(x, shift, scale, gate, p)
    return x


def kernel(x, c, w_ada, b_ada, norm_g, w_in, q_norm_g, k_norm_g, sinks, dw_w, dw_b, ln_g, ln_b, pw2_w, pw2_b, sconv_w, a_log, dt_bias, dn_norm_g, w_proj_a, w_proj_b, w_proj_c, w_out):
    return _forward(x, c, w_ada, b_ada, norm_g, w_in, q_norm_g, k_norm_g, sinks, dw_w, dw_b, ln_g, ln_b,
                    pw2_w, pw2_b, sconv_w, a_log, dt_bias, dn_norm_g, w_proj_a, w_proj_b, w_proj_c, w_out)
```

```python
import functools

import numpy as np
import jax
import jax.numpy as jnp
from jax import lax
from jax.experimental import pallas as pl
from jax.experimental.pallas import tpu as pltpu

F32 = jnp.float32
BF16 = jnp.bfloat16

D_MODEL = 1024
ATT_HEADS = 8
ATT_KV_HEADS = 2
ATT_HEAD_DIM = 64
ATT_GROUP = ATT_HEADS // ATT_KV_HEADS
ATT_WIDTH = ATT_HEADS * ATT_HEAD_DIM
ATT_KV_WIDTH = ATT_KV_HEADS * ATT_HEAD_DIM
GROUP_WIDTH = ATT_GROUP * ATT_HEAD_DIM
WINDOW = 128
CONV_WIDTH = 512
CONV_K = 31
DN_HEADS = 4
DN_HEAD_DIM = 128
DN_WIDTH = DN_HEADS * DN_HEAD_DIM
DN_CONV_K = 4
DN_CHUNK = 64
EPS = 1e-6
NEG_INF = -1e30

LANES = 128
SUBLANES = 8

TILE = 256
CONV_PAD = 32
SCONV_PAD = 8
CONV_ROWS = 32
VMEM_LIMIT = 56 * 1024 * 1024

OFF_A = 0
OFF_B = OFF_A + 2 * ATT_WIDTH + 2 * ATT_KV_WIDTH
OFF_C = OFF_B + 3 * CONV_WIDTH
OFF_ZC = OFF_C + 3 * DN_WIDTH
OFF_MG = OFF_ZC + DN_WIDTH
OFF_AB = OFF_MG + 3 * D_MODEL
W_COLS = OFF_AB + LANES


def _dot(a, b, precision=None):
    return jnp.dot(a, b, preferred_element_type=F32, precision=precision)


def _dot_nt(a, b):
    return lax.dot_general(a, b, (((1,), (1,)), ((), ())), preferred_element_type=F32)


def _silu(x):
    return x * jax.nn.sigmoid(x)


def _softplus(x):
    return jnp.maximum(x, 0.0) + jnp.log1p(jnp.exp(-jnp.abs(x)))


def _ada_body(c_ref, w_ref, b_ref, o_ref):
    sc = _silu(c_ref[...]).astype(BF16)
    o_ref[...] = _dot(sc, w_ref[...].astype(BF16)) + b_ref[...]


def _ada_call(c_pad, w_ada, b_ada):
    depth, d, d3 = w_ada.shape
    nblk = d3 // d
    return pl.pallas_call(
        _ada_body,
        grid=(depth, nblk),
        in_specs=[
            pl.BlockSpec((SUBLANES, d), lambda l, j: (0, 0)),
            pl.BlockSpec((None, d, d), lambda l, j: (l, 0, j)),
            pl.BlockSpec((None, 1, d), lambda l, j: (l, 0, j)),
        ],
        out_specs=pl.BlockSpec((None, SUBLANES, d), lambda l, j: (l, 0, j)),
        out_shape=jax.ShapeDtypeStruct((depth, SUBLANES, d3), F32),
        name="adaln_mod",
    )(c_pad, w_ada, b_ada.reshape(depth, 1, d3))


def _layer_body(x_ref, shift_ref, scale_ref, gate_ref, ng_ref, w_ref, qg_ref, kg_ref,
                sink_ref, bias_ref, dww_ref, dwb_ref, lng_ref, lnb_ref, pw2w_ref, pw2b_ref,
                scw_ref, alog_r_ref, dtb_r_ref, dng_ref,
                wpa_ref, wpb_ref, wpc_ref, wout_ref,
                o_ref,
                kprev_ref, vprev_ref, convbuf_ref, sconvbuf_ref, state_ref,
                ya_ref, phase_ref, cv_ref, qkv_ref, yc_ref):
    tm = x_ref.shape[0]
    t = pl.program_id(1)

    @pl.when(t == 0)
    def _reset():
        kprev_ref[...] = jnp.zeros_like(kprev_ref)
        vprev_ref[...] = jnp.zeros_like(vprev_ref)
        convbuf_ref[0:CONV_PAD, :] = jnp.zeros((CONV_PAD, CONV_WIDTH), F32)
        sconvbuf_ref[0:SCONV_PAD, :] = jnp.zeros((SCONV_PAD, 3 * DN_WIDTH), F32)
        state_ref[...] = jnp.zeros_like(state_ref)

    x = x_ref[...]
    ms = jnp.mean(x * x, axis=-1, keepdims=True)
    h = x * lax.rsqrt(ms + EPS) * ng_ref[...]
    h = h * (1.0 + scale_ref[...]) + shift_ref[...]
    hb = h.astype(BF16)

    def proj(lo, hi):
        return _dot(hb, w_ref[:, lo:hi])

    pa = proj(OFF_A, OFF_B)
    qa = pa[:, 0:ATT_WIDTH]
    ka = pa[:, ATT_WIDTH:ATT_WIDTH + ATT_KV_WIDTH]
    va = pa[:, ATT_WIDTH + ATT_KV_WIDTH:ATT_WIDTH + 2 * ATT_KV_WIDTH]
    za = pa[:, ATT_WIDTH + 2 * ATT_KV_WIDTH:]

    def head_mean_square(v, width):
        r = lax.broadcasted_iota(jnp.int32, (width, width), 0) // ATT_HEAD_DIM
        c = lax.broadcasted_iota(jnp.int32, (width, width), 1) // ATT_HEAD_DIM
        avg = jnp.where(r == c, 1.0 / ATT_HEAD_DIM, 0.0).astype(BF16)
        sq = v * v
        hi = sq.astype(BF16)
        lo = (sq - hi.astype(F32)).astype(BF16)
        return _dot(hi, avg) + _dot(lo, avg)

    qn = qa * lax.rsqrt(head_mean_square(qa, ATT_WIDTH) + EPS) * qg_ref[...] * (ATT_HEAD_DIM ** -0.5)
    kn = ka * lax.rsqrt(head_mean_square(ka, ATT_KV_WIDTH) + EPS) * kg_ref[...]

    low_half = lax.broadcasted_iota(jnp.int32, (tm, ATT_KV_WIDTH), 1) < ATT_HEAD_DIM

    def replicate(v):
        swapped = pltpu.roll(v, ATT_HEAD_DIM, axis=1)
        head0 = jnp.where(low_half, v, swapped)
        head1 = jnp.where(low_half, swapped, v)
        return [jnp.concatenate([hd, hd], axis=1).astype(BF16) for hd in (head0, head1)]

    krep = replicate(kn)
    vrep = replicate(va)

    q_head = lax.broadcasted_iota(jnp.int32, (WINDOW, GROUP_WIDTH), 1) // ATT_HEAD_DIM
    v_head = lax.broadcasted_iota(jnp.int32, (2 * WINDOW, GROUP_WIDTH), 1) // ATT_HEAD_DIM
    key_col = lax.broadcasted_iota(jnp.int32, (ATT_GROUP * WINDOW, 2 * WINDOW), 1)
    n_masked_cols = jnp.where(t == 0, WINDOW, 0)

    for j in range(tm // WINDOW):
        rows = slice(j * WINDOW, (j + 1) * WINDOW)
        for g in range(ATT_KV_HEADS):
            if j == 0:
                k_prev, v_prev = kprev_ref[g], vprev_ref[g]
            else:
                prev_rows = slice((j - 1) * WINDOW, j * WINDOW)
                k_prev, v_prev = krep[g][prev_rows], vrep[g][prev_rows]
            kcat = jnp.concatenate([k_prev, krep[g][rows]], axis=0)
            vcat = jnp.concatenate([v_prev, vrep[g][rows]], axis=0)
            qg = qn[rows, g * GROUP_WIDTH:(g + 1) * GROUP_WIDTH]
            qstack = jnp.concatenate(
                [jnp.where(q_head == i, qg, 0.0) for i in range(ATT_GROUP)], axis=0).astype(BF16)
            s = _dot_nt(qstack, kcat) + bias_ref[g]
            if j == 0:
                s = jnp.where(key_col < n_masked_cols, NEG_INF, s)
            sink = sink_ref[g]
            m = jnp.maximum(jnp.max(s, axis=-1, keepdims=True), sink)
            p = jnp.exp(s - m)
            denom = jnp.sum(p, axis=-1, keepdims=True) + jnp.exp(sink - m)
            p = (p / denom).astype(BF16)
            o = None
            for i in range(ATT_GROUP):
                v_i = jnp.where(v_head == i, vcat, jnp.zeros_like(vcat))
                o_i = _dot(p[i * WINDOW:(i + 1) * WINDOW], v_i)
                o = o_i if o is None else o + o_i
            ya_ref[rows, g * GROUP_WIDTH:(g + 1) * GROUP_WIDTH] = o

    for g in range(ATT_KV_HEADS):
        kprev_ref[g] = krep[g][tm - WINDOW:tm]
        vprev_ref[g] = vrep[g][tm - WINDOW:tm]

    ya = ya_ref[...] * _silu(za)
    merged = _dot(ya.astype(BF16), wpa_ref[...]) * jax.nn.sigmoid(proj(OFF_MG, OFF_MG + D_MODEL))

    pb = proj(OFF_B, OFF_C)
    ub = pb[:, 0:CONV_WIDTH] * jax.nn.sigmoid(pb[:, CONV_WIDTH:2 * CONV_WIDTH])
    zb = pb[:, 2 * CONV_WIDTH:]
    convbuf_ref[CONV_PAD:CONV_PAD + tm, :] = ub
    base = CONV_PAD - (CONV_K - 1)
    phase_rows = phase_ref.shape[1]
    for ph in range(1, SUBLANES):
        phase_ref[ph - 1] = convbuf_ref[ph:ph + phase_rows, :]
    for r in range(tm // CONV_ROWS):
        acc = jnp.broadcast_to(dwb_ref[...], (CONV_ROWS, CONV_WIDTH))
        for k in range(CONV_K):
            ph = (base + k) % SUBLANES
            start = (base + k) - ph + r * CONV_ROWS
            if ph == 0:
                win = convbuf_ref[start:start + CONV_ROWS, :]
            else:
                win = phase_ref[ph - 1, start:start + CONV_ROWS, :]
            acc = acc + dww_ref[k:k + 1, :] * win
        cv_ref[r * CONV_ROWS:(r + 1) * CONV_ROWS, :] = acc
    convbuf_ref[0:CONV_PAD, :] = convbuf_ref[tm:tm + CONV_PAD, :]
    cv = cv_ref[...]
    mu = jnp.mean(cv, axis=-1, keepdims=True)
    xc = cv - mu
    var = jnp.mean(xc * xc, axis=-1, keepdims=True)
    ln = xc * lax.rsqrt(var + EPS) * lng_ref[...] + lnb_ref[...]
    yb = (_dot(_silu(ln).astype(BF16), pw2w_ref[...]) + pw2b_ref[...]) * _silu(zb)
    merged = merged + _dot(yb.astype(BF16), wpb_ref[...]) * jax.nn.sigmoid(
        proj(OFF_MG + D_MODEL, OFF_MG + 2 * D_MODEL))

    sconvbuf_ref[SCONV_PAD:SCONV_PAD + tm, :] = proj(OFF_C, OFF_ZC)
    sbase = SCONV_PAD - (DN_CONV_K - 1)
    for r in range(tm // CONV_ROWS):
        for cblk in range(3):
            cols = slice(cblk * DN_WIDTH, (cblk + 1) * DN_WIDTH)
            acc = None
            for k in range(DN_CONV_K):
                start = sbase + k + r * CONV_ROWS
                term = scw_ref[k:k + 1, cols] * sconvbuf_ref[start:start + CONV_ROWS, cols]
                acc = term if acc is None else acc + term
            qkv_ref[r * CONV_ROWS:(r + 1) * CONV_ROWS, cols] = _silu(acc)
    sconvbuf_ref[0:SCONV_PAD, :] = sconvbuf_ref[tm:tm + SCONV_PAD, :]

    ab = proj(OFF_AB, W_COLS)
    g_col = -jnp.exp(alog_r_ref[...]) * _softplus(ab + dtb_r_ref[...])
    beta_col = jax.nn.sigmoid(ab)

    n_chunks = tm // DN_CHUNK
    ri = lax.broadcasted_iota(jnp.int32, (tm, tm), 0)
    ci = lax.broadcasted_iota(jnp.int32, (tm, tm), 1)
    same_chunk = (ri // DN_CHUNK) == (ci // DN_CHUNK)
    hp = lax.Precision.HIGHEST
    gc_col = _dot(jnp.where(same_chunk & (ri >= ci), 1.0, 0.0), g_col, precision=hp)
    gl_col = _dot(jnp.where(same_chunk, 1.0, 0.0), g_col, precision=hp)
    gc_t = [gc_col[c * DN_CHUNK:(c + 1) * DN_CHUNK].T for c in range(n_chunks)]

    rc = lax.broadcasted_iota(jnp.int32, (DN_CHUNK, DN_CHUNK), 0)
    cc = lax.broadcasted_iota(jnp.int32, (DN_CHUNK, DN_CHUNK), 1)
    lower = rc >= cc
    strict = rc > cc
    heads = range(DN_HEADS)
    chunks = range(n_chunks)
    crows = [slice(c * DN_CHUNK, (c + 1) * DN_CHUNK) for c in chunks]

    q_l, k_l, kb_l, qe_l, ke_l, xin_l, gc_l, gl_l = [], [], [], [], [], [], [], []
    for hd in heads:
        qh = qkv_ref[:, hd * DN_HEAD_DIM:(hd + 1) * DN_HEAD_DIM]
        kh = qkv_ref[:, DN_WIDTH + hd * DN_HEAD_DIM:DN_WIDTH + (hd + 1) * DN_HEAD_DIM]
        vh = qkv_ref[:, 2 * DN_WIDTH + hd * DN_HEAD_DIM:2 * DN_WIDTH + (hd + 1) * DN_HEAD_DIM]
        qh = qh * lax.rsqrt(jnp.sum(qh * qh, axis=-1, keepdims=True) + EPS) * (DN_HEAD_DIM ** -0.5)
        kh = kh * lax.rsqrt(jnp.sum(kh * kh, axis=-1, keepdims=True) + EPS)
        beta = beta_col[:, DN_HEADS + hd:DN_HEADS + hd + 1]
        gc = gc_col[:, hd:hd + 1]
        gl = gl_col[:, hd:hd + 1]
        kb = kh * beta
        eg = jnp.exp(gc)
        q_l.append(qh)
        k_l.append(kh)
        kb_l.append(kb)
        qe_l.append(qh * eg)
        ke_l.append(kh * jnp.exp(gl - gc))
        xin_l.append(jnp.concatenate([vh * beta, kb * eg], axis=1))
        gc_l.append(gc)
        gl_l.append(gl)

    a_blk, intra_blk = {}, {}
    for hd in heads:
        for c in chunks:
            kq = jnp.concatenate([kb_l[hd][crows[c]], q_l[hd][crows[c]]], axis=0).astype(BF16)
            sc = _dot_nt(kq, k_l[hd][crows[c]].astype(BF16))
            decay = jnp.exp(jnp.where(lower, gc_l[hd][crows[c]] - gc_t[c][hd:hd + 1, :], NEG_INF))
            a_blk[hd, c] = jnp.where(strict, sc[0:DN_CHUNK] * decay, 0.0)
            intra_blk[hd, c] = (sc[DN_CHUNK:] * decay).astype(BF16)

    keys = [(hd, c) for hd in heads for c in chunks]
    pw_blk = dict(a_blk)
    q_blk = {kk: -a_blk[kk] for kk in keys}
    for _ in range(int(np.log2(DN_CHUNK)) - 1):
        pw_bf = {kk: pw_blk[kk].astype(BF16) for kk in keys}
        pw_blk = {kk: _dot(pw_bf[kk], pw_bf[kk]) for kk in keys}
        q_blk = {kk: q_blk[kk] + pw_blk[kk] + _dot(q_blk[kk].astype(BF16), pw_blk[kk].astype(BF16))
                 for kk in keys}
    uw_blk = {}
    for hd, c in keys:
        xin = xin_l[hd][crows[c]]
        uw_blk[hd, c] = xin + _dot(q_blk[hd, c].astype(BF16), xin.astype(BF16))

    state = [state_ref[hd] for hd in heads]
    od_rows = [[] for _ in heads]
    for c in chunks:
        res = []
        for hd in heads:
            wq = jnp.concatenate([uw_blk[hd, c][:, DN_HEAD_DIM:], qe_l[hd][crows[c]]], axis=0).astype(BF16)
            res.append(_dot(wq, state[hd].astype(BF16)))
        for hd in heads:
            v_new = uw_blk[hd, c][:, 0:DN_HEAD_DIM] - res[hd][0:DN_CHUNK]
            v_new_bf = v_new.astype(BF16)
            od_rows[hd].append(res[hd][DN_CHUNK:] + _dot(intra_blk[hd, c], v_new_bf))
            g_last = gl_l[hd][c * DN_CHUNK:c * DN_CHUNK + 1, :]
            state[hd] = state[hd] * jnp.exp(g_last) + _dot(ke_l[hd][crows[c]].T.astype(BF16), v_new_bf)
    for hd in heads:
        state_ref[hd] = state[hd]
        od = jnp.concatenate(od_rows[hd], axis=0)
        od = od * lax.rsqrt(jnp.mean(od * od, axis=-1, keepdims=True) + EPS) * dng_ref[...]
        yc_ref[:, hd * DN_HEAD_DIM:(hd + 1) * DN_HEAD_DIM] = od

    yc = yc_ref[...] * _silu(proj(OFF_ZC, OFF_MG))
    merged = merged + _dot(yc.astype(BF16), wpc_ref[...]) * jax.nn.sigmoid(
        proj(OFF_MG + 2 * D_MODEL, OFF_AB))

    o_ref[...] = x + gate_ref[...] * _dot(merged.astype(BF16), wout_ref[...])


def _const_spec(shape):
    nd = len(shape)
    return pl.BlockSpec(shape, lambda b, t: (0,) * nd, pipeline_mode=pl.Buffered(1))


def _alibi_bias():
    qi = np.arange(WINDOW)[:, None]
    kj = np.arange(2 * WINDOW)[None, :]
    dist = qi + WINDOW - kj
    valid = (dist >= 0) & (dist < WINDOW)
    slopes = np.exp2(-8.0 * np.arange(1, ATT_HEADS + 1, dtype=np.float32) / ATT_HEADS)
    bias = np.where(valid[None], -slopes[:, None, None] * dist[None].astype(np.float32), NEG_INF)
    return jnp.asarray(bias.reshape(ATT_KV_HEADS, ATT_GROUP * WINDOW, 2 * WINDOW), dtype=F32)


def _layer_call(x, shift, scale, gate, p):
    bsz, seq, d = x.shape
    tm = TILE
    assert seq % tm == 0 and d == D_MODEL
    consts = [p["norm_g"], p["w"], p["q_norm_g"], p["k_norm_g"], p["sinks"], p["bias"],
              p["dw_w"], p["dw_b"], p["ln_g"], p["ln_b"], p["pw2_w"], p["pw2_b"], p["sconv_w"],
              p["alog_r"], p["dtb_r"], p["dn_norm_g"],
              p["w_proj_a"], p["w_proj_b"], p["w_proj_c"], p["w_out"]]
    tile_spec = pl.BlockSpec((None, tm, d), lambda b, t: (b, t, 0))
    mod_spec = pl.BlockSpec((None, 1, d), lambda b, t: (b, 0, 0))
    return pl.pallas_call(
        _layer_body,
        grid=(bsz, seq // tm),
        in_specs=[tile_spec, mod_spec, mod_spec, mod_spec] + [_const_spec(a.shape) for a in consts],
        out_specs=tile_spec,
        out_shape=jax.ShapeDtypeStruct(x.shape, x.dtype),
        scratch_shapes=[
            pltpu.VMEM((ATT_KV_HEADS, WINDOW, GROUP_WIDTH), BF16),
            pltpu.VMEM((ATT_KV_HEADS, WINDOW, GROUP_WIDTH), BF16),
            pltpu.VMEM((CONV_PAD + tm, CONV_WIDTH), F32),
            pltpu.VMEM((SCONV_PAD + tm, 3 * DN_WIDTH), F32),
            pltpu.VMEM((DN_HEADS, DN_HEAD_DIM, DN_HEAD_DIM), F32),
            pltpu.VMEM((tm, ATT_WIDTH), F32),
            pltpu.VMEM((SUBLANES - 1, tm + CONV_PAD - SUBLANES, CONV_WIDTH), F32),
            pltpu.VMEM((tm, CONV_WIDTH), F32),
            pltpu.VMEM((tm, 3 * DN_WIDTH), F32),
            pltpu.VMEM((tm, DN_WIDTH), F32),
        ],
        compiler_params=pltpu.CompilerParams(
            dimension_semantics=("arbitrary", "arbitrary"),
            vmem_limit_bytes=VMEM_LIMIT),
        name="hybrid_layer",
    )(x, shift, scale, gate, *consts)


def _prep_layer(l, w_in, norm_g, q_norm_g, k_norm_g, sinks, dw_w, dw_b, ln_g, ln_b, pw2_w, pw2_b,
                sconv_w, a_log, dt_bias, dn_norm_g, w_proj_a, w_proj_b, w_proj_c, w_out):
    d = w_in.shape[1]
    wl = w_in[l]
    ab_lo = OFF_ZC
    ab_hi = ab_lo + 2 * DN_HEADS
    w_ab = wl[:, ab_lo:ab_hi]
    w = jnp.concatenate(
        [wl[:, :ab_lo], wl[:, ab_hi:], w_ab, jnp.zeros((d, LANES - 2 * DN_HEADS), wl.dtype)],
        axis=1).astype(BF16)

    def lane_pad(v):
        return jnp.zeros((1, LANES), F32).at[0, :DN_HEADS].set(v)

    return dict(
        norm_g=norm_g[l][None, :],
        w=w,
        q_norm_g=jnp.tile(q_norm_g[l], ATT_HEADS)[None, :],
        k_norm_g=jnp.tile(k_norm_g[l], ATT_KV_HEADS)[None, :],
        sinks=jnp.repeat(sinks[l], WINDOW).reshape(ATT_KV_HEADS, ATT_GROUP * WINDOW, 1),
        bias=_alibi_bias(),
        dw_w=jnp.concatenate([dw_w[l], jnp.zeros((1, CONV_WIDTH), F32)], axis=0),
        dw_b=dw_b[l][None, :],
        ln_g=ln_g[l][None, :],
        ln_b=ln_b[l][None, :],
        pw2_w=pw2_w[l].astype(BF16),
        pw2_b=pw2_b[l][None, :],
        sconv_w=jnp.concatenate([sconv_w[l], jnp.zeros((SUBLANES - DN_CONV_K, 3 * DN_WIDTH), F32)], axis=0),
        alog_r=lane_pad(a_log[l]),
        dtb_r=lane_pad(dt_bias[l]),
        dn_norm_g=dn_norm_g[l][None, :],
        w_proj_a=w_proj_a[l].astype(BF16),
        w_proj_b=w_proj_b[l].astype(BF16),
        w_proj_c=w_proj_c[l].astype(BF16),
        w_out=w_out[l].astype(BF16),
    )


@jax.jit
def _forward(x, c, w_ada, b_ada, norm_g, w_in, q_norm_g, k_norm_g, sinks, dw_w, dw_b, ln_g, ln_b,
             pw2_w, pw2_b, sconv_w, a_log, dt_bias, dn_norm_g, w_proj_a, w_proj_b, w_proj_c, w_out):
    bsz, _, d = x.shape
    depth = w_ada.shape[0]
    c_pad = jnp.zeros((SUBLANES, d), F32).at[:bsz].set(c)
    mod = _ada_call(c_pad, w_ada, b_ada)
    for l in range(depth):
        p = _prep_layer(l, w_in, norm_g, q_norm_g, k_norm_g, sinks, dw_w, dw_b, ln_g, ln_b, pw2_w,
                        pw2_b, sconv_w, a_log, dt_bias, dn_norm_g, w_proj_a, w_proj_b, w_proj_c, w_out)
        ml = mod[l, :bsz]
        shift = ml[:, None, 0:d]
        scale = ml[:, None, d:2 * d]
        gate = ml[:, None, 2 * d:3 * d]
        x = _layer_call(x, shift, scale, gate, p)
    return x


def kernel(x, c, w_ada, b_ada, norm_g, w_in, q_norm_g, k_norm_g, sinks, dw_w, dw_b, ln_g, ln_b, pw2_w, pw2_b, sconv_w, a_log, dt_bias, dn_norm_g, w_proj_a, w_proj_b, w_proj_c, w_out):
    return _forward(x, c, w_ada, b_ada, norm_g, w_in, q_norm_g, k_norm_g, sinks, dw_w, dw_b, ln_g, ln_b,
                    pw2_w, pw2_b, sconv_w, a_log, dt_bias, dn_norm_g, w_proj_a, w_proj_b, w_proj_c, w_out)
```

```python
import functools

import numpy as np
import jax
import jax.numpy as jnp
from jax import lax
from jax.experimental import pallas as pl
from jax.experimental.pallas import tpu as pltpu

F32 = jnp.float32
BF16 = jnp.bfloat16

D_MODEL = 1024
ATT_HEADS = 8
ATT_KV_HEADS = 2
ATT_HEAD_DIM = 64
ATT_GROUP = ATT_HEADS // ATT_KV_HEADS
ATT_WIDTH = ATT_HEADS * ATT_HEAD_DIM
ATT_KV_WIDTH = ATT_KV_HEADS * ATT_HEAD_DIM
GROUP_WIDTH = ATT_GROUP * ATT_HEAD_DIM
WINDOW = 128
CONV_WIDTH = 512
CONV_K = 31
DN_HEADS = 4
DN_HEAD_DIM = 128
DN_WIDTH = DN_HEADS * DN_HEAD_DIM
DN_CONV_K = 4
DN_CHUNK = 64
EPS = 1e-6
NEG_INF = -1e30

LANES = 128
SUBLANES = 8
MXU_COLS = 256

TILE = 256
CONV_PAD = 32
SCONV_PAD = 8
CONV_ROWS = 32
VMEM_LIMIT = 56 * 1024 * 1024

OFF_A = 0
OFF_B = OFF_A + 2 * ATT_WIDTH + 2 * ATT_KV_WIDTH
OFF_C = OFF_B + 3 * CONV_WIDTH
OFF_ZC = OFF_C + 3 * DN_WIDTH
OFF_MG = OFF_ZC + DN_WIDTH
OFF_AB = OFF_MG + 3 * D_MODEL
W_COLS = OFF_AB + LANES


def _dot(a, b, precision=None):
    return jnp.dot(a, b, preferred_element_type=F32, precision=precision)


def _dot_cols(a, w_ref, lo=0, hi=None):
    hi = w_ref.shape[1] if hi is None else hi
    n = hi - lo
    if n < 2 * MXU_COLS:
        return _dot(a, w_ref[:, lo:hi])
    mid = lo + -(-n // (2 * MXU_COLS)) * MXU_COLS
    return jnp.concatenate([_dot(a, w_ref[:, lo:mid]), _dot(a, w_ref[:, mid:hi])], axis=1)


def _dot_nt(a, b):
    return lax.dot_general(a, b, (((1,), (1,)), ((), ())), preferred_element_type=F32)


def _silu(x):
    return x * jax.nn.sigmoid(x)


def _softplus(x):
    return jnp.maximum(x, 0.0) + jnp.log1p(jnp.exp(-jnp.abs(x)))


def _ada_body(c_ref, w_ref, b_ref, o_ref):
    sc = _silu(c_ref[...]).astype(BF16)
    o_ref[...] = _dot(sc, w_ref[...].astype(BF16)) + b_ref[...]


def _ada_call(c_pad, w_ada, b_ada):
    depth, d, d3 = w_ada.shape
    nblk = d3 // d
    return pl.pallas_call(
        _ada_body,
        grid=(depth, nblk),
        in_specs=[
            pl.BlockSpec((SUBLANES, d), lambda l, j: (0, 0)),
            pl.BlockSpec((None, d, d), lambda l, j: (l, 0, j)),
            pl.BlockSpec((None, 1, d), lambda l, j: (l, 0, j)),
        ],
        out_specs=pl.BlockSpec((None, SUBLANES, d), lambda l, j: (l, 0, j)),
        out_shape=jax.ShapeDtypeStruct((depth, SUBLANES, d3), F32),
        name="adaln_mod",
    )(c_pad, w_ada, b_ada.reshape(depth, 1, d3))


def _layer_body(x_ref, shift_ref, scale_ref, gate_ref, ng_ref, w_ref, qg_ref, kg_ref,
                sink_ref, bias_ref, dww_ref, dwb_ref, lng_ref, lnb_ref, pw2w_ref, pw2b_ref,
                scw_ref, alog_r_ref, dtb_r_ref, dng_ref,
                wpa_ref, wpb_ref, wpc_ref, wout_ref,
                o_ref,
                kprev_ref, vprev_ref, convbuf_ref, sconvbuf_ref, state_ref,
                ya_ref, phase_ref, cv_ref, qkv_ref, yc_ref, gates_ref):
    tm = x_ref.shape[0]
    t = pl.program_id(1)

    @pl.when(t == 0)
    def _reset():
        kprev_ref[...] = jnp.zeros_like(kprev_ref)
        vprev_ref[...] = jnp.zeros_like(vprev_ref)
        convbuf_ref[0:CONV_PAD, :] = jnp.zeros((CONV_PAD, CONV_WIDTH), F32)
        sconvbuf_ref[0:SCONV_PAD, :] = jnp.zeros((SCONV_PAD, 3 * DN_WIDTH), F32)
        state_ref[...] = jnp.zeros_like(state_ref)

    x = x_ref[...]
    ms = jnp.mean(x * x, axis=-1, keepdims=True)
    h = x * lax.rsqrt(ms + EPS) * ng_ref[...]
    h = h * (1.0 + scale_ref[...]) + shift_ref[...]
    hb = h.astype(BF16)

    def proj(lo, hi):
        return _dot_cols(hb, w_ref, lo, hi)

    pa = proj(OFF_A, OFF_B)
    qa = pa[:, 0:ATT_WIDTH]
    ka = pa[:, ATT_WIDTH:ATT_WIDTH + ATT_KV_WIDTH]
    va = pa[:, ATT_WIDTH + ATT_KV_WIDTH:ATT_WIDTH + 2 * ATT_KV_WIDTH]
    za = pa[:, ATT_WIDTH + 2 * ATT_KV_WIDTH:]

    def head_mean_square(v, width):
        r = lax.broadcasted_iota(jnp.int32, (width, width), 0) // ATT_HEAD_DIM
        c = lax.broadcasted_iota(jnp.int32, (width, width), 1) // ATT_HEAD_DIM
        avg = jnp.where(r == c, 1.0 / ATT_HEAD_DIM, 0.0).astype(BF16)
        sq = v * v
        hi = sq.astype(BF16)
        lo = (sq - hi.astype(F32)).astype(BF16)
        return _dot(hi, avg) + _dot(lo, avg)

    qn = qa * lax.rsqrt(head_mean_square(qa, ATT_WIDTH) + EPS) * qg_ref[...] * (ATT_HEAD_DIM ** -0.5)
    kn = ka * lax.rsqrt(head_mean_square(ka, ATT_KV_WIDTH) + EPS) * kg_ref[...]

    low_half = lax.broadcasted_iota(jnp.int32, (tm, ATT_KV_WIDTH), 1) < ATT_HEAD_DIM

    def replicate(v):
        swapped = pltpu.roll(v, ATT_HEAD_DIM, axis=1)
        head0 = jnp.where(low_half, v, swapped)
        head1 = jnp.where(low_half, swapped, v)
        return [jnp.concatenate([hd, hd], axis=1).astype(BF16) for hd in (head0, head1)]

    krep = replicate(kn)
    vrep = replicate(va)

    q_head = lax.broadcasted_iota(jnp.int32, (WINDOW, GROUP_WIDTH), 1) // ATT_HEAD_DIM
    v_head = lax.broadcasted_iota(jnp.int32, (2 * WINDOW, GROUP_WIDTH), 1) // ATT_HEAD_DIM
    key_col = lax.broadcasted_iota(jnp.int32, (ATT_GROUP * WINDOW, 2 * WINDOW), 1)
    n_masked_cols = jnp.where(t == 0, WINDOW, 0)

    blocks = [(j, g) for j in range(tm // WINDOW) for g in range(ATT_KV_HEADS)]
    scores, vcats = {}, {}
    for j, g in blocks:
        rows = slice(j * WINDOW, (j + 1) * WINDOW)
        if j == 0:
            k_prev, v_prev = kprev_ref[g], vprev_ref[g]
        else:
            prev_rows = slice((j - 1) * WINDOW, j * WINDOW)
            k_prev, v_prev = krep[g][prev_rows], vrep[g][prev_rows]
        kcat = jnp.concatenate([k_prev, krep[g][rows]], axis=0)
        vcats[j, g] = jnp.concatenate([v_prev, vrep[g][rows]], axis=0)
        qg = qn[rows, g * GROUP_WIDTH:(g + 1) * GROUP_WIDTH]
        qstack = jnp.concatenate(
            [jnp.where(q_head == i, qg, 0.0) for i in range(ATT_GROUP)], axis=0).astype(BF16)
        s = _dot_nt(qstack, kcat) + bias_ref[g]
        if j == 0:
            s = jnp.where(key_col < n_masked_cols, NEG_INF, s)
        scores[j, g] = s
    probs = {}
    for j, g in blocks:
        s = scores[j, g]
        sink = sink_ref[g]
        m = jnp.maximum(jnp.max(s, axis=-1, keepdims=True), sink)
        p = jnp.exp(s - m)
        denom = jnp.sum(p, axis=-1, keepdims=True) + jnp.exp(sink - m)
        probs[j, g] = (p * (1.0 / denom)).astype(BF16)
    for j, g in blocks:
        o = None
        for i in range(ATT_GROUP):
            v_i = jnp.where(v_head == i, vcats[j, g], jnp.zeros_like(vcats[j, g]))
            o_i = _dot(probs[j, g][i * WINDOW:(i + 1) * WINDOW], v_i)
            o = o_i if o is None else o + o_i
        ya_ref[j * WINDOW:(j + 1) * WINDOW, g * GROUP_WIDTH:(g + 1) * GROUP_WIDTH] = o

    for g in range(ATT_KV_HEADS):
        kprev_ref[g] = krep[g][tm - WINDOW:tm]
        vprev_ref[g] = vrep[g][tm - WINDOW:tm]

    ya = ya_ref[...] * _silu(za)

    pb = proj(OFF_B, OFF_C)
    ub = pb[:, 0:CONV_WIDTH] * jax.nn.sigmoid(pb[:, CONV_WIDTH:2 * CONV_WIDTH])
    zb = pb[:, 2 * CONV_WIDTH:]
    convbuf_ref[CONV_PAD:CONV_PAD + tm, :] = ub
    base = CONV_PAD - (CONV_K - 1)
    phase_rows = phase_ref.shape[1]
    for ph in range(1, SUBLANES):
        phase_ref[ph - 1] = convbuf_ref[ph:ph + phase_rows, :]
    sconvbuf_ref[SCONV_PAD:SCONV_PAD + tm, :] = proj(OFF_C, OFF_ZC)
    for piece in range(6):
        gates_ref[:, piece * 512:(piece + 1) * 512] = jax.nn.sigmoid(
            proj(OFF_MG + piece * 512, OFF_MG + (piece + 1) * 512))
    for r in range(tm // CONV_ROWS):
        acc = jnp.broadcast_to(dwb_ref[...], (CONV_ROWS, CONV_WIDTH))
        for k in range(CONV_K):
            ph = (base + k) % SUBLANES
            start = (base + k) - ph + r * CONV_ROWS
            if ph == 0:
                win = convbuf_ref[start:start + CONV_ROWS, :]
            else:
                win = phase_ref[ph - 1, start:start + CONV_ROWS, :]
            acc = acc + jnp.concatenate([dww_ref[k]] * (CONV_ROWS // SUBLANES), axis=0) * win
        cv_ref[r * CONV_ROWS:(r + 1) * CONV_ROWS, :] = acc
    convbuf_ref[0:CONV_PAD, :] = convbuf_ref[tm:tm + CONV_PAD, :]
    cv = cv_ref[...]
    mu = jnp.mean(cv, axis=-1, keepdims=True)
    xc = cv - mu
    var = jnp.mean(xc * xc, axis=-1, keepdims=True)
    ln = xc * lax.rsqrt(var + EPS) * lng_ref[...] + lnb_ref[...]
    yb = (_dot_cols(_silu(ln).astype(BF16), pw2w_ref) + pw2b_ref[...]) * _silu(zb)
    merged = _dot_cols(ya.astype(BF16), wpa_ref) * gates_ref[:, 0:D_MODEL]
    merged = merged + _dot_cols(yb.astype(BF16), wpb_ref) * gates_ref[:, D_MODEL:2 * D_MODEL]

    sbase = SCONV_PAD - (DN_CONV_K - 1)
    for r in range(tm // CONV_ROWS):
        for cblk in range(3):
            cols = slice(cblk * DN_WIDTH, (cblk + 1) * DN_WIDTH)
            acc = None
            for k in range(DN_CONV_K):
                start = sbase + k + r * CONV_ROWS
                tap = jnp.concatenate([scw_ref[k, :, cols]] * (CONV_ROWS // SUBLANES), axis=0)
                term = tap * sconvbuf_ref[start:start + CONV_ROWS, cols]
                acc = term if acc is None else acc + term
            qkv_ref[r * CONV_ROWS:(r + 1) * CONV_ROWS, cols] = _silu(acc)
    sconvbuf_ref[0:SCONV_PAD, :] = sconvbuf_ref[tm:tm + SCONV_PAD, :]

    ab = proj(OFF_AB, W_COLS)
    g_col = -jnp.exp(alog_r_ref[...]) * _softplus(ab + dtb_r_ref[...])
    beta_col = jax.nn.sigmoid(ab)

    n_chunks = tm // DN_CHUNK
    ri = lax.broadcasted_iota(jnp.int32, (tm, tm), 0)
    ci = lax.broadcasted_iota(jnp.int32, (tm, tm), 1)
    same_chunk = (ri // DN_CHUNK) == (ci // DN_CHUNK)
    hp = lax.Precision.HIGHEST
    gc_col = _dot(jnp.where(same_chunk & (ri >= ci), 1.0, 0.0), g_col, precision=hp)
    gl_col = _dot(jnp.where(same_chunk, 1.0, 0.0), g_col, precision=hp)
    gc_t = [gc_col[c * DN_CHUNK:(c + 1) * DN_CHUNK].T for c in range(n_chunks)]

    rc = lax.broadcasted_iota(jnp.int32, (DN_CHUNK, DN_CHUNK), 0)
    cc = lax.broadcasted_iota(jnp.int32, (DN_CHUNK, DN_CHUNK), 1)
    lower = rc >= cc
    strict = rc > cc
    heads = range(DN_HEADS)
    chunks = range(n_chunks)
    crows = [slice(c * DN_CHUNK, (c + 1) * DN_CHUNK) for c in chunks]

    q_l, k_l, kb_l, qe_l, ke_l, xin_l, gc_l, gl_l = [], [], [], [], [], [], [], []
    for hd in heads:
        qh = qkv_ref[:, hd * DN_HEAD_DIM:(hd + 1) * DN_HEAD_DIM]
        kh = qkv_ref[:, DN_WIDTH + hd * DN_HEAD_DIM:DN_WIDTH + (hd + 1) * DN_HEAD_DIM]
        vh = qkv_ref[:, 2 * DN_WIDTH + hd * DN_HEAD_DIM:2 * DN_WIDTH + (hd + 1) * DN_HEAD_DIM]
        qh = qh * lax.rsqrt(jnp.sum(qh * qh, axis=-1, keepdims=True) + EPS) * (DN_HEAD_DIM ** -0.5)
        kh = kh * lax.rsqrt(jnp.sum(kh * kh, axis=-1, keepdims=True) + EPS)
        beta = beta_col[:, DN_HEADS + hd:DN_HEADS + hd + 1]
        gc = gc_col[:, hd:hd + 1]
        gl = gl_col[:, hd:hd + 1]
        kb = kh * beta
        eg = jnp.exp(gc)
        q_l.append(qh)
        k_l.append(kh)
        kb_l.append(kb)
        qe_l.append(qh * eg)
        ke_l.append(kh * jnp.exp(gl - gc))
        xin_l.append(jnp.concatenate([vh * beta, kb * eg], axis=1))
        gc_l.append(gc)
        gl_l.append(gl)

    a_blk, intra_blk = {}, {}
    for hd in heads:
        for c in chunks:
            kq = jnp.concatenate([kb_l[hd][crows[c]], q_l[hd][crows[c]]], axis=0).astype(BF16)
            sc = _dot_nt(kq, k_l[hd][crows[c]].astype(BF16))
            decay = jnp.exp(jnp.where(lower, gc_l[hd][crows[c]] - gc_t[c][hd:hd + 1, :], NEG_INF))
            a_blk[hd, c] = jnp.where(strict, sc[0:DN_CHUNK] * decay, 0.0)
            intra_blk[hd, c] = (sc[DN_CHUNK:] * decay).astype(BF16)

    keys = [(hd, c) for hd in heads for c in chunks]
    pw_blk = dict(a_blk)
    q_blk = {kk: -a_blk[kk] for kk in keys}
    for _ in range(int(np.log2(DN_CHUNK)) - 1):
        pw_bf = {kk: pw_blk[kk].astype(BF16) for kk in keys}
        pw_blk = {kk: _dot(pw_bf[kk], pw_bf[kk]) for kk in keys}
        q_blk = {kk: q_blk[kk] + pw_blk[kk] + _dot(q_blk[kk].astype(BF16), pw_blk[kk].astype(BF16))
                 for kk in keys}
    uw_blk = {}
    for hd, c in keys:
        xin = xin_l[hd][crows[c]]
        uw_blk[hd, c] = xin + _dot(q_blk[hd, c].astype(BF16), xin.astype(BF16))

    state = [state_ref[hd] for hd in heads]
    od_rows = [[] for _ in heads]
    for c in chunks:
        res = []
        for hd in heads:
            wq = jnp.concatenate([uw_blk[hd, c][:, DN_HEAD_DIM:], qe_l[hd][crows[c]]], axis=0).astype(BF16)
            res.append(_dot(wq, state[hd].astype(BF16)))
        for hd in heads:
            v_new = uw_blk[hd, c][:, 0:DN_HEAD_DIM] - res[hd][0:DN_CHUNK]
            v_new_bf = v_new.astype(BF16)
            od_rows[hd].append(res[hd][DN_CHUNK:] + _dot(intra_blk[hd, c], v_new_bf))
            g_last = gl_l[hd][c * DN_CHUNK:c * DN_CHUNK + 1, :]
            state[hd] = state[hd] * jnp.exp(g_last) + _dot(ke_l[hd][crows[c]].T.astype(BF16), v_new_bf)
    for hd in heads:
        state_ref[hd] = state[hd]
        od = jnp.concatenate(od_rows[hd], axis=0)
        od = od * lax.rsqrt(jnp.mean(od * od, axis=-1, keepdims=True) + EPS) * dng_ref[...]
        yc_ref[:, hd * DN_HEAD_DIM:(hd + 1) * DN_HEAD_DIM] = od

    yc = yc_ref[...] * _silu(proj(OFF_ZC, OFF_MG))
    merged = merged + _dot_cols(yc.astype(BF16), wpc_ref) * gates_ref[:, 2 * D_MODEL:3 * D_MODEL]

    o_ref[...] = x + gate_ref[...] * _dot_cols(merged.astype(BF16), wout_ref)


def _const_spec(shape):
    nd = len(shape)
    return pl.BlockSpec(shape, lambda b, t: (0,) * nd, pipeline_mode=pl.Buffered(1))


def _alibi_bias():
    qi = np.arange(WINDOW)[:, None]
    kj = np.arange(2 * WINDOW)[None, :]
    dist = qi + WINDOW - kj
    valid = (dist >= 0) & (dist < WINDOW)
    slopes = np.exp2(-8.0 * np.arange(1, ATT_HEADS + 1, dtype=np.float32) / ATT_HEADS)
    bias = np.where(valid[None], -slopes[:, None, None] * dist[None].astype(np.float32), NEG_INF)
    return jnp.asarray(bias.reshape(ATT_KV_HEADS, ATT_GROUP * WINDOW, 2 * WINDOW), dtype=F32)


def _layer_call(x, shift, scale, gate, p):
    bsz, seq, d = x.shape
    tm = TILE
    assert seq % tm == 0 and d == D_MODEL
    consts = [p["norm_g"], p["w"], p["q_norm_g"], p["k_norm_g"], p["sinks"], p["bias"],
              p["dw_w"], p["dw_b"], p["ln_g"], p["ln_b"], p["pw2_w"], p["pw2_b"], p["sconv_w"],
              p["alog_r"], p["dtb_r"], p["dn_norm_g"],
              p["w_proj_a"], p["w_proj_b"], p["w_proj_c"], p["w_out"]]
    tile_spec = pl.BlockSpec((None, tm, d), lambda b, t: (b, t, 0))
    mod_spec = pl.BlockSpec((None, 1, d), lambda b, t: (b, 0, 0))
    return pl.pallas_call(
        _layer_body,
        grid=(bsz, seq // tm),
        in_specs=[tile_spec, mod_spec, mod_spec, mod_spec] + [_const_spec(a.shape) for a in consts],
        out_specs=tile_spec,
        out_shape=jax.ShapeDtypeStruct(x.shape, x.dtype),
        scratch_shapes=[
            pltpu.VMEM((ATT_KV_HEADS, WINDOW, GROUP_WIDTH), BF16),
            pltpu.VMEM((ATT_KV_HEADS, WINDOW, GROUP_WIDTH), BF16),
            pltpu.VMEM((CONV_PAD + tm, CONV_WIDTH), F32),
            pltpu.VMEM((SCONV_PAD + tm, 3 * DN_WIDTH), F32),
            pltpu.VMEM((DN_HEADS, DN_HEAD_DIM, DN_HEAD_DIM), F32),
            pltpu.VMEM((tm, ATT_WIDTH), F32),
            pltpu.VMEM((SUBLANES - 1, tm + CONV_PAD - SUBLANES, CONV_WIDTH), F32),
            pltpu.VMEM((tm, CONV_WIDTH), F32),
            pltpu.VMEM((tm, 3 * DN_WIDTH), F32),
            pltpu.VMEM((tm, DN_WIDTH), F32),
            pltpu.VMEM((tm, 3 * D_MODEL), F32),
        ],
        compiler_params=pltpu.CompilerParams(
            dimension_semantics=("arbitrary", "arbitrary"),
            vmem_limit_bytes=VMEM_LIMIT),
        name="hybrid_layer",
    )(x, shift, scale, gate, *consts)


def _prep_layer(l, w_in, norm_g, q_norm_g, k_norm_g, sinks, dw_w, dw_b, ln_g, ln_b, pw2_w, pw2_b,
                sconv_w, a_log, dt_bias, dn_norm_g, w_proj_a, w_proj_b, w_proj_c, w_out):
    d = w_in.shape[1]
    wl = w_in[l]
    ab_lo = OFF_ZC
    ab_hi = ab_lo + 2 * DN_HEADS
    w_ab = wl[:, ab_lo:ab_hi]
    w = jnp.concatenate(
        [wl[:, :ab_lo], wl[:, ab_hi:], w_ab, jnp.zeros((d, LANES - 2 * DN_HEADS), wl.dtype)],
        axis=1).astype(BF16)

    def lane_pad(v):
        return jnp.zeros((1, LANES), F32).at[0, :DN_HEADS].set(v)

    return dict(
        norm_g=norm_g[l][None, :],
        w=w,
        q_norm_g=jnp.tile(q_norm_g[l], ATT_HEADS)[None, :],
        k_norm_g=jnp.tile(k_norm_g[l], ATT_KV_HEADS)[None, :],
        sinks=jnp.repeat(sinks[l], WINDOW).reshape(ATT_KV_HEADS, ATT_GROUP * WINDOW, 1),
        bias=_alibi_bias(),
        dw_w=jnp.broadcast_to(dw_w[l][:, None, :], (CONV_K, SUBLANES, CONV_WIDTH)),
        dw_b=dw_b[l][None, :],
        ln_g=ln_g[l][None, :],
        ln_b=ln_b[l][None, :],
        pw2_w=pw2_w[l].astype(BF16),
        pw2_b=pw2_b[l][None, :],
        sconv_w=jnp.broadcast_to(sconv_w[l][:, None, :], (DN_CONV_K, SUBLANES, 3 * DN_WIDTH)),
        alog_r=lane_pad(a_log[l]),
        dtb_r=lane_pad(dt_bias[l]),
        dn_norm_g=dn_norm_g[l][None, :],
        w_proj_a=w_proj_a[l].astype(BF16),
        w_proj_b=w_proj_b[l].astype(BF16),
        w_proj_c=w_proj_c[l].astype(BF16),
        w_out=w_out[l].astype(BF16),
    )


@jax.jit
def _forward(x, c, w_ada, b_ada, norm_g, w_in, q_norm_g, k_norm_g, sinks, dw_w, dw_b, ln_g, ln_b,
             pw2_w, pw2_b, sconv_w, a_log, dt_bias, dn_norm_g, w_proj_a, w_proj_b, w_proj_c, w_out):
    bsz, _, d = x.shape
    depth = w_ada.shape[0]
    c_pad = jnp.zeros((SUBLANES, d), F32).at[:bsz].set(c)
    mod = _ada_call(c_pad, w_ada, b_ada)
    for l in range(depth):
        p = _prep_layer(l, w_in, norm_g, q_norm_g, k_norm_g, sinks, dw_w, dw_b, ln_g, ln_b, pw2_w,
                        pw2_b, sconv_w, a_log, dt_bias, dn_norm_g, w_proj_a, w_proj_b, w_proj_c, w_out)
        ml = mod[l, :bsz]
        shift = ml[:, None, 0:d]
        scale = ml[:, None, d:2 * d]
        gate = ml[:, None, 2 * d:3 * d]
        x = _layer_call(x, shift, scale, gate, p)
    return x


def kernel(x, c, w_ada, b_ada, norm_g, w_in, q_norm_g, k_norm_g, sinks, dw_w, dw_b, ln_g, ln_b, pw2_w, pw2_b, sconv_w, a_log, dt_bias, dn_norm_g, w_proj_a, w_proj_b, w_proj_c, w_out):
    return _forward(x, c, w_ada, b_ada, norm_g, w_in, q_norm_g, k_norm_g, sinks, dw_w, dw_b, ln_g, ln_b,
                    pw2_w, pw2_b, sconv_w, a_log, dt_bias, dn_norm_g, w_proj_a, w_proj_b, w_proj_c, w_out)
```

```python
import numpy as np
import jax
import jax.numpy as jnp
from jax import lax
from jax.experimental import pallas as pl
from jax.experimental.pallas import tpu as pltpu

F32 = jnp.float32
BF16 = jnp.bfloat16

D_MODEL = 1024
ATT_HEADS = 8
ATT_KV_HEADS = 2
ATT_HEAD_DIM = 64
ATT_GROUP = ATT_HEADS // ATT_KV_HEADS
ATT_WIDTH = ATT_HEADS * ATT_HEAD_DIM
ATT_KV_WIDTH = ATT_KV_HEADS * ATT_HEAD_DIM
GROUP_WIDTH = ATT_GROUP * ATT_HEAD_DIM
WINDOW = 128
CONV_WIDTH = 512
CONV_K = 31
DN_HEADS = 4
DN_HEAD_DIM = 128
DN_WIDTH = DN_HEADS * DN_HEAD_DIM
DN_CONV_K = 4
DN_CHUNK = 64
EPS = 1e-6
NEG_INF = -1e30

LANES = 128
SUBLANES = 8

TILE = 256
CONV_PAD = 32
SCONV_PAD = 8
CONV_ROWS = 32
GATE_COLS = 512
VMEM_LIMIT = 56 * 1024 * 1024

OFF_A = 0
OFF_B = OFF_A + 2 * ATT_WIDTH + 2 * ATT_KV_WIDTH
OFF_C = OFF_B + 3 * CONV_WIDTH
MAIN_COLS = OFF_C + 3 * DN_WIDTH
AB_COLS = 2 * DN_HEADS
TAIL_ZC = 0
TAIL_MG = TAIL_ZC + DN_WIDTH
TAIL_COLS = TAIL_MG + 3 * D_MODEL

(V_NORM_G, V_Q_GAIN, V_K_GAIN, V_DW_B, V_LN_G, V_LN_B, V_PW2_B, V_A_LOG, V_DT_BIAS, V_DN_GAIN,
 N_VECS) = range(11)


def _dot(a, b, precision=None):
    return jnp.dot(a, b, preferred_element_type=F32, precision=precision)


def _dot_nt(a, b):
    return lax.dot_general(a, b, (((1,), (1,)), ((), ())), preferred_element_type=F32)


def _silu(x):
    return x * jax.nn.sigmoid(x)


def _softplus(x):
    return jnp.maximum(x, 0.0) + jnp.log1p(jnp.exp(-jnp.abs(x)))


def _ada_body(c_ref, w_ref, b_ref, o_ref):
    sc = _silu(c_ref[...]).astype(BF16)
    o_ref[...] = _dot(sc, w_ref[...].astype(BF16)) + b_ref[...]


def _ada_call(c_pad, w_ada, b_ada):
    depth, d, d3 = w_ada.shape
    nblk = d3 // d
    return pl.pallas_call(
        _ada_body,
        grid=(depth, nblk),
        in_specs=[
            pl.BlockSpec((SUBLANES, d), lambda l, j: (0, 0)),
            pl.BlockSpec((None, d, d), lambda l, j: (l, 0, j)),
            pl.BlockSpec((None, 1, d), lambda l, j: (l, 0, j)),
        ],
        out_specs=pl.BlockSpec((None, SUBLANES, d), lambda l, j: (l, 0, j)),
        out_shape=jax.ShapeDtypeStruct((depth, SUBLANES, d3), F32),
        name="adaln_mod",
    )(c_pad, w_ada, b_ada.reshape(depth, 1, d3))


def _layer_body(x_ref, mod_ref, vec_ref, wmain_ref, wtail_ref, wab_ref, sink_ref, bias_ref,
                dww_ref, pw2w_ref, scw_ref, wpa_ref, wpb_ref, wpc_ref, wout_ref,
                o_ref,
                kprev_ref, vprev_ref, convbuf_ref, sconvbuf_ref, state_ref,
                ya_ref, phase_ref, cv_ref, qkv_ref, yc_ref, gates_ref):
    tm = x_ref.shape[0]
    t = pl.program_id(1)

    @pl.when(t == 0)
    def _reset():
        kprev_ref[...] = jnp.zeros_like(kprev_ref)
        vprev_ref[...] = jnp.zeros_like(vprev_ref)
        convbuf_ref[0:CONV_PAD, :] = jnp.zeros((CONV_PAD, CONV_WIDTH), F32)
        sconvbuf_ref[0:SCONV_PAD, :] = jnp.zeros((SCONV_PAD, 3 * DN_WIDTH), F32)
        state_ref[...] = jnp.zeros_like(state_ref)

    def vec(row, width):
        return vec_ref[row, :, 0:width]

    mod = mod_ref[pl.ds(pl.program_id(0), 1), :]
    shift = mod[:, 0:D_MODEL]
    scale = mod[:, D_MODEL:2 * D_MODEL]
    gate = mod[:, 2 * D_MODEL:3 * D_MODEL]
    x = x_ref[...]
    ms = jnp.mean(x * x, axis=-1, keepdims=True)
    h = x * lax.rsqrt(ms + EPS) * vec(V_NORM_G, D_MODEL)
    h = h * (1.0 + scale) + shift
    hb = h.astype(BF16)

    def proj(w_ref, lo, hi):
        return _dot(hb, w_ref[:, lo:hi])

    pa = proj(wmain_ref, OFF_A, OFF_B)
    qa = pa[:, 0:ATT_WIDTH]
    ka = pa[:, ATT_WIDTH:ATT_WIDTH + ATT_KV_WIDTH]
    va = pa[:, ATT_WIDTH + ATT_KV_WIDTH:ATT_WIDTH + 2 * ATT_KV_WIDTH]
    za = pa[:, ATT_WIDTH + 2 * ATT_KV_WIDTH:]

    def head_mean_square(v, width):
        r = lax.broadcasted_iota(jnp.int32, (width, width), 0) // ATT_HEAD_DIM
        c = lax.broadcasted_iota(jnp.int32, (width, width), 1) // ATT_HEAD_DIM
        avg = jnp.where(r == c, 1.0 / ATT_HEAD_DIM, 0.0).astype(BF16)
        sq = v * v
        hi = sq.astype(BF16)
        lo = (sq - hi.astype(F32)).astype(BF16)
        return _dot(hi, avg) + _dot(lo, avg)

    qn = (qa * lax.rsqrt(head_mean_square(qa, ATT_WIDTH) + EPS) * vec(V_Q_GAIN, ATT_WIDTH)
          * (ATT_HEAD_DIM ** -0.5))
    kn = ka * lax.rsqrt(head_mean_square(ka, ATT_KV_WIDTH) + EPS) * vec(V_K_GAIN, ATT_KV_WIDTH)

    low_half = lax.broadcasted_iota(jnp.int32, (tm, ATT_KV_WIDTH), 1) < ATT_HEAD_DIM

    def replicate(v):
        swapped = pltpu.roll(v, ATT_HEAD_DIM, axis=1)
        head0 = jnp.where(low_half, v, swapped)
        head1 = jnp.where(low_half, swapped, v)
        return [jnp.concatenate([hd, hd], axis=1).astype(BF16) for hd in (head0, head1)]

    krep = replicate(kn)
    vrep = replicate(va)

    q_head = lax.broadcasted_iota(jnp.int32, (WINDOW, GROUP_WIDTH), 1) // ATT_HEAD_DIM
    v_head = lax.broadcasted_iota(jnp.int32, (2 * WINDOW, GROUP_WIDTH), 1) // ATT_HEAD_DIM
    key_col = lax.broadcasted_iota(jnp.int32, (ATT_GROUP * WINDOW, 2 * WINDOW), 1)
    n_masked_cols = jnp.where(t == 0, WINDOW, 0)

    blocks = [(j, g) for j in range(tm // WINDOW) for g in range(ATT_KV_HEADS)]
    scores, vcats = {}, {}
    for j, g in blocks:
        rows = slice(j * WINDOW, (j + 1) * WINDOW)
        if j == 0:
            k_prev, v_prev = kprev_ref[g], vprev_ref[g]
        else:
            prev_rows = slice((j - 1) * WINDOW, j * WINDOW)
            k_prev, v_prev = krep[g][prev_rows], vrep[g][prev_rows]
        kcat = jnp.concatenate([k_prev, krep[g][rows]], axis=0)
        vcats[j, g] = jnp.concatenate([v_prev, vrep[g][rows]], axis=0)
        qg = qn[rows, g * GROUP_WIDTH:(g + 1) * GROUP_WIDTH]
        qstack = jnp.concatenate(
            [jnp.where(q_head == i, qg, 0.0) for i in range(ATT_GROUP)], axis=0).astype(BF16)
        s = _dot_nt(qstack, kcat) + bias_ref[g]
        if j == 0:
            s = jnp.where(key_col < n_masked_cols, NEG_INF, s)
        scores[j, g] = s
    probs = {}
    for j, g in blocks:
        s = scores[j, g]
        sink = sink_ref[g]
        m = jnp.maximum(jnp.max(s, axis=-1, keepdims=True), sink)
        p = jnp.exp(s - m)
        denom = jnp.sum(p, axis=-1, keepdims=True) + jnp.exp(sink - m)
        probs[j, g] = (p * (1.0 / denom)).astype(BF16)
    for j, g in blocks:
        o = None
        for i in range(ATT_GROUP):
            v_i = jnp.where(v_head == i, vcats[j, g], jnp.zeros_like(vcats[j, g]))
            o_i = _dot(probs[j, g][i * WINDOW:(i + 1) * WINDOW], v_i)
            o = o_i if o is None else o + o_i
        ya_ref[j * WINDOW:(j + 1) * WINDOW, g * GROUP_WIDTH:(g + 1) * GROUP_WIDTH] = o

    for g in range(ATT_KV_HEADS):
        kprev_ref[g] = krep[g][tm - WINDOW:tm]
        vprev_ref[g] = vrep[g][tm - WINDOW:tm]

    ya = ya_ref[...] * _silu(za)

    pb = proj(wmain_ref, OFF_B, OFF_C)
    ub = pb[:, 0:CONV_WIDTH] * jax.nn.sigmoid(pb[:, CONV_WIDTH:2 * CONV_WIDTH])
    zb = pb[:, 2 * CONV_WIDTH:]
    convbuf_ref[CONV_PAD:CONV_PAD + tm, :] = ub
    base = CONV_PAD - (CONV_K - 1)
    phase_rows = phase_ref.shape[1]
    for ph in range(1, SUBLANES):
        phase_ref[ph - 1] = convbuf_ref[ph:ph + phase_rows, :]
    sconvbuf_ref[SCONV_PAD:SCONV_PAD + tm, :] = proj(wmain_ref, OFF_C, MAIN_COLS)
    for piece in range(3 * D_MODEL // GATE_COLS):
        cols = slice(piece * GATE_COLS, (piece + 1) * GATE_COLS)
        gates_ref[:, cols] = jax.nn.sigmoid(
            proj(wtail_ref, TAIL_MG + piece * GATE_COLS, TAIL_MG + (piece + 1) * GATE_COLS))
    dw_b = vec(V_DW_B, CONV_WIDTH)
    for r in range(tm // CONV_ROWS):
        acc = jnp.broadcast_to(dw_b, (CONV_ROWS, CONV_WIDTH))
        for k in range(CONV_K):
            ph = (base + k) % SUBLANES
            start = (base + k) - ph + r * CONV_ROWS
            if ph == 0:
                win = convbuf_ref[start:start + CONV_ROWS, :]
            else:
                win = phase_ref[ph - 1, start:start + CONV_ROWS, :]
            acc = acc + jnp.concatenate([dww_ref[k]] * (CONV_ROWS // SUBLANES), axis=0) * win
        cv_ref[r * CONV_ROWS:(r + 1) * CONV_ROWS, :] = acc
    convbuf_ref[0:CONV_PAD, :] = convbuf_ref[tm:tm + CONV_PAD, :]
    cv = cv_ref[...]
    mu = jnp.mean(cv, axis=-1, keepdims=True)
    xc = cv - mu
    var = jnp.mean(xc * xc, axis=-1, keepdims=True)
    ln = xc * lax.rsqrt(var + EPS) * vec(V_LN_G, CONV_WIDTH) + vec(V_LN_B, CONV_WIDTH)
    yb = (_dot(_silu(ln).astype(BF16), pw2w_ref[...]) + vec(V_PW2_B, CONV_WIDTH)) * _silu(zb)
    merged = _dot(ya.astype(BF16), wpa_ref[...]) * gates_ref[:, 0:D_MODEL]
    merged = merged + _dot(yb.astype(BF16), wpb_ref[...]) * gates_ref[:, D_MODEL:2 * D_MODEL]

    sbase = SCONV_PAD - (DN_CONV_K - 1)
    for r in range(tm // CONV_ROWS):
        for cblk in range(3):
            cols = slice(cblk * DN_WIDTH, (cblk + 1) * DN_WIDTH)
            acc = None
            for k in range(DN_CONV_K):
                start = sbase + k + r * CONV_ROWS
                tap = jnp.concatenate([scw_ref[k, :, cols]] * (CONV_ROWS // SUBLANES), axis=0)
                term = tap * sconvbuf_ref[start:start + CONV_ROWS, cols]
                acc = term if acc is None else acc + term
            qkv_ref[r * CONV_ROWS:(r + 1) * CONV_ROWS, cols] = _silu(acc)
    sconvbuf_ref[0:SCONV_PAD, :] = sconvbuf_ref[tm:tm + SCONV_PAD, :]

    ab = _dot(hb, wab_ref[...])
    g_col = -jnp.exp(vec(V_A_LOG, LANES)) * _softplus(ab + vec(V_DT_BIAS, LANES))
    beta_col = jax.nn.sigmoid(ab)

    n_chunks = tm // DN_CHUNK
    ri = lax.broadcasted_iota(jnp.int32, (tm, tm), 0)
    ci = lax.broadcasted_iota(jnp.int32, (tm, tm), 1)
    same_chunk = (ri // DN_CHUNK) == (ci // DN_CHUNK)
    hp = lax.Precision.HIGHEST
    gc_col = _dot(jnp.where(same_chunk & (ri >= ci), 1.0, 0.0), g_col, precision=hp)
    gl_col = _dot(jnp.where(same_chunk, 1.0, 0.0), g_col, precision=hp)
    gc_row = gc_col.T

    heads = range(DN_HEADS)
    chunks = range(n_chunks)
    crows = [slice(c * DN_CHUNK, (c + 1) * DN_CHUNK) for c in chunks]
    cat_w = n_chunks * DN_CHUNK
    cat_row = lax.broadcasted_iota(jnp.int32, (DN_CHUNK, cat_w), 0)
    cat_lane = lax.broadcasted_iota(jnp.int32, (DN_CHUNK, cat_w), 1)
    cat_blk = cat_lane // DN_CHUNK
    cat_col = cat_lane - cat_blk * DN_CHUNK
    lower = cat_row >= cat_col
    strict = cat_row > cat_col
    diag_blk = (lax.broadcasted_iota(jnp.int32, (cat_w, cat_w), 0) // DN_CHUNK
                == lax.broadcasted_iota(jnp.int32, (cat_w, cat_w), 1) // DN_CHUNK)
    key_blk = (lax.broadcasted_iota(jnp.int32, (tm, n_chunks * DN_HEAD_DIM), 0) // DN_CHUNK
               == lax.broadcasted_iota(jnp.int32, (tm, n_chunks * DN_HEAD_DIM), 1) // DN_HEAD_DIM)

    def block_diag(m_cat):
        tiled = jnp.concatenate([m_cat] * n_chunks, axis=0)
        return jnp.where(diag_blk, tiled, jnp.zeros_like(tiled))

    def chunk_cat(v):
        return jnp.concatenate([v[crows[c]] for c in chunks], axis=1)

    qe_l, ke_l, xin_l, gl_l, sc_l, decay_l = [], [], [], [], [], []
    for hd in heads:
        qh = qkv_ref[:, hd * DN_HEAD_DIM:(hd + 1) * DN_HEAD_DIM]
        kh = qkv_ref[:, DN_WIDTH + hd * DN_HEAD_DIM:DN_WIDTH + (hd + 1) * DN_HEAD_DIM]
        vh = qkv_ref[:, 2 * DN_WIDTH + hd * DN_HEAD_DIM:2 * DN_WIDTH + (hd + 1) * DN_HEAD_DIM]
        qh = qh * lax.rsqrt(jnp.sum(qh * qh, axis=-1, keepdims=True) + EPS) * (DN_HEAD_DIM ** -0.5)
        kh = kh * lax.rsqrt(jnp.sum(kh * kh, axis=-1, keepdims=True) + EPS)
        beta = beta_col[:, DN_HEADS + hd:DN_HEADS + hd + 1]
        gc = gc_col[:, hd:hd + 1]
        gl = gl_col[:, hd:hd + 1]
        kb = kh * beta
        eg = jnp.exp(gc)
        qe_l.append(qh * eg)
        ke_l.append(kh * jnp.exp(gl - gc))
        xin_l.append(jnp.concatenate([vh * beta, kb * eg], axis=1))
        gl_l.append(gl)
        kq = jnp.concatenate([chunk_cat(kb), chunk_cat(qh)], axis=0).astype(BF16)
        k_tiled = jnp.concatenate([kh.astype(BF16)] * n_chunks, axis=1)
        k_bd = jnp.where(key_blk, k_tiled, jnp.zeros_like(k_tiled))
        sc_l.append(_dot_nt(kq, k_bd))
        gc_i = gc[crows[n_chunks - 1]]
        for c in reversed(range(n_chunks - 1)):
            gc_i = jnp.where(cat_blk == c, gc[crows[c]], gc_i)
        decay_l.append(jnp.exp(jnp.where(lower, gc_i - gc_row[hd:hd + 1, :], NEG_INF)))

    pw_l = [jnp.where(strict, sc_l[hd][0:DN_CHUNK] * decay_l[hd], 0.0) for hd in heads]
    intra_l = [(sc_l[hd][DN_CHUNK:] * decay_l[hd]).astype(BF16) for hd in heads]
    q_l = [-pw_l[hd] for hd in heads]
    n_steps = int(np.log2(DN_CHUNK))
    for step in range(n_steps):
        bd_l = [block_diag(pw_l[hd].astype(BF16)) for hd in heads]
        if step == 0:
            pw_l = [_dot(pw_l[hd].astype(BF16), bd_l[hd]) for hd in heads]
            continue
        last = step == n_steps - 1
        lhs_l = [q_l[hd].astype(BF16) if last else
                 jnp.concatenate([pw_l[hd], q_l[hd]], axis=0).astype(BF16) for hd in heads]
        r_l = [_dot(lhs_l[hd], bd_l[hd]) for hd in heads]
        q_l = [q_l[hd] + pw_l[hd] + r_l[hd][-DN_CHUNK:] for hd in heads]
        if not last:
            pw_l = [r_l[hd][0:DN_CHUNK] for hd in heads]
    uw_l = [xin_l[hd] + _dot(block_diag(q_l[hd].astype(BF16)), xin_l[hd].astype(BF16)) for hd in heads]

    state = [state_ref[hd] for hd in heads]
    o_inter = [[] for _ in heads]
    v_news = [[] for _ in heads]
    for c in chunks:
        res = []
        for hd in heads:
            wq = jnp.concatenate([uw_l[hd][crows[c], DN_HEAD_DIM:], qe_l[hd][crows[c]]], axis=0).astype(BF16)
            res.append(_dot(wq, state[hd].astype(BF16)))
        for hd in heads:
            v_new = uw_l[hd][crows[c], 0:DN_HEAD_DIM] - res[hd][0:DN_CHUNK]
            v_news[hd].append(v_new)
            o_inter[hd].append(res[hd][DN_CHUNK:])
            g_last = gl_l[hd][c * DN_CHUNK:c * DN_CHUNK + 1, :]
            state[hd] = state[hd] * jnp.exp(g_last) + _dot(ke_l[hd][crows[c]].T.astype(BF16),
                                                           v_new.astype(BF16))
    dn_gain = vec(V_DN_GAIN, DN_HEAD_DIM)
    for hd in heads:
        state_ref[hd] = state[hd]
        v_new_all = jnp.concatenate(v_news[hd], axis=0).astype(BF16)
        od = jnp.concatenate(o_inter[hd], axis=0) + _dot(block_diag(intra_l[hd]), v_new_all)
        od = od * lax.rsqrt(jnp.mean(od * od, axis=-1, keepdims=True) + EPS) * dn_gain
        yc_ref[:, hd * DN_HEAD_DIM:(hd + 1) * DN_HEAD_DIM] = od

    yc = yc_ref[...] * _silu(proj(wtail_ref, TAIL_ZC, TAIL_MG))
    merged = merged + _dot(yc.astype(BF16), wpc_ref[...]) * gates_ref[:, 2 * D_MODEL:3 * D_MODEL]

    o_ref[...] = x + gate * _dot(merged.astype(BF16), wout_ref[...])


def _alibi_bias():
    qi = np.arange(WINDOW)[:, None]
    kj = np.arange(2 * WINDOW)[None, :]
    dist = qi + WINDOW - kj
    valid = (dist >= 0) & (dist < WINDOW)
    slopes = np.exp2(-8.0 * np.arange(1, ATT_HEADS + 1, dtype=np.float32) / ATT_HEADS)
    bias = np.where(valid[None], -slopes[:, None, None] * dist[None].astype(np.float32), NEG_INF)
    return jnp.asarray(bias.reshape(ATT_KV_HEADS, ATT_GROUP * WINDOW, 2 * WINDOW), dtype=F32)


def _layer_call(layer, x, mod, p):
    bsz, seq, d = x.shape
    tm = TILE
    assert seq % tm == 0 and d == D_MODEL and bsz <= SUBLANES

    def layer_spec(a):
        nd = a.ndim - 1
        return pl.BlockSpec((None,) + a.shape[1:], lambda b, t: (layer,) + (0,) * nd,
                            pipeline_mode=pl.Buffered(1))

    def const_spec(a):
        nd = a.ndim
        return pl.BlockSpec(a.shape, lambda b, t: (0,) * nd, pipeline_mode=pl.Buffered(1))

    tile_spec = pl.BlockSpec((None, tm, d), lambda b, t: (b, t, 0))
    per_layer = [mod, p["vecs"], p["w_main"], p["w_tail"], p["w_ab"], p["sinks"]]
    after_bias = [p["dw_w"], p["pw2_w"], p["sconv_w"], p["w_proj_a"], p["w_proj_b"], p["w_proj_c"],
                  p["w_out"]]
    return pl.pallas_call(
        _layer_body,
        grid=(bsz, seq // tm),
        in_specs=([tile_spec] + [layer_spec(a) for a in per_layer] + [const_spec(p["bias"])]
                  + [layer_spec(a) for a in after_bias]),
        out_specs=tile_spec,
        out_shape=jax.ShapeDtypeStruct(x.shape, x.dtype),
        scratch_shapes=[
            pltpu.VMEM((ATT_KV_HEADS, WINDOW, GROUP_WIDTH), BF16),
            pltpu.VMEM((ATT_KV_HEADS, WINDOW, GROUP_WIDTH), BF16),
            pltpu.VMEM((CONV_PAD + tm, CONV_WIDTH), F32),
            pltpu.VMEM((SCONV_PAD + tm, 3 * DN_WIDTH), F32),
            pltpu.VMEM((DN_HEADS, DN_HEAD_DIM, DN_HEAD_DIM), F32),
            pltpu.VMEM((tm, ATT_WIDTH), F32),
            pltpu.VMEM((SUBLANES - 1, tm + CONV_PAD - SUBLANES, CONV_WIDTH), F32),
            pltpu.VMEM((tm, CONV_WIDTH), F32),
            pltpu.VMEM((tm, 3 * DN_WIDTH), F32),
            pltpu.VMEM((tm, DN_WIDTH), F32),
            pltpu.VMEM((tm, 3 * D_MODEL), F32),
        ],
        compiler_params=pltpu.CompilerParams(
            dimension_semantics=("arbitrary", "arbitrary"),
            vmem_limit_bytes=VMEM_LIMIT),
        name="hybrid_layer",
    )(x, *per_layer, p["bias"], *after_bias)


def _prep_params(norm_g, w_in, q_norm_g, k_norm_g, sinks, dw_w, dw_b, ln_g, ln_b, pw2_w, pw2_b,
                 sconv_w, a_log, dt_bias, dn_norm_g, w_proj_a, w_proj_b, w_proj_c, w_out):
    depth, d, _ = w_in.shape

    def row(v):
        return jnp.pad(v, ((0, 0), (0, D_MODEL - v.shape[1])))[:, None, :]

    vec_rows = [None] * N_VECS
    vec_rows[V_NORM_G] = row(norm_g)
    vec_rows[V_Q_GAIN] = row(jnp.tile(q_norm_g, (1, ATT_HEADS)))
    vec_rows[V_K_GAIN] = row(jnp.tile(k_norm_g, (1, ATT_KV_HEADS)))
    vec_rows[V_DW_B] = row(dw_b)
    vec_rows[V_LN_G] = row(ln_g)
    vec_rows[V_LN_B] = row(ln_b)
    vec_rows[V_PW2_B] = row(pw2_b)
    vec_rows[V_A_LOG] = row(a_log)
    vec_rows[V_DT_BIAS] = row(dt_bias)
    vec_rows[V_DN_GAIN] = row(dn_norm_g)
    ab_hi = MAIN_COLS + AB_COLS
    return dict(
        vecs=jnp.stack(vec_rows, axis=1),
        w_main=w_in[:, :, :MAIN_COLS].astype(BF16),
        w_tail=w_in[:, :, ab_hi:].astype(BF16),
        w_ab=jnp.pad(w_in[:, :, MAIN_COLS:ab_hi], ((0, 0), (0, 0), (0, LANES - AB_COLS))).astype(BF16),
        sinks=jnp.repeat(sinks, WINDOW, axis=1).reshape(depth, ATT_KV_HEADS, ATT_GROUP * WINDOW, 1),
        bias=_alibi_bias(),
        dw_w=jnp.broadcast_to(dw_w[:, :, None, :], (depth, CONV_K, SUBLANES, CONV_WIDTH)),
        pw2_w=pw2_w.astype(BF16),
        sconv_w=jnp.broadcast_to(sconv_w[:, :, None, :], (depth, DN_CONV_K, SUBLANES, 3 * DN_WIDTH)),
        w_proj_a=w_proj_a.astype(BF16),
        w_proj_b=w_proj_b.astype(BF16),
        w_proj_c=w_proj_c.astype(BF16),
        w_out=w_out.astype(BF16),
    )


@jax.jit
def _forward(x, c, w_ada, b_ada, norm_g, w_in, q_norm_g, k_norm_g, sinks, dw_w, dw_b, ln_g, ln_b,
             pw2_w, pw2_b, sconv_w, a_log, dt_bias, dn_norm_g, w_proj_a, w_proj_b, w_proj_c, w_out):
    bsz, _, d = x.shape
    depth = w_ada.shape[0]
    c_pad = jnp.zeros((SUBLANES, d), F32).at[:bsz].set(c)
    mod = _ada_call(c_pad, w_ada, b_ada)
    p = _prep_params(norm_g, w_in, q_norm_g, k_norm_g, sinks, dw_w, dw_b, ln_g, ln_b, pw2_w, pw2_b,
                     sconv_w, a_log, dt_bias, dn_norm_g, w_proj_a, w_proj_b, w_proj_c, w_out)
    for layer in range(depth):
        x = _layer_call(layer, x, mod, p)
    return x


def kernel(x, c, w_ada, b_ada, norm_g, w_in, q_norm_g, k_norm_g, sinks, dw_w, dw_b, ln_g, ln_b, pw2_w, pw2_b, sconv_w, a_log, dt_bias, dn_norm_g, w_proj_a, w_proj_b, w_proj_c, w_out):
    return _forward(x, c, w_ada, b_ada, norm_g, w_in, q_norm_g, k_norm_g, sinks, dw_w, dw_b, ln_g, ln_b,
                    pw2_w, pw2_b, sconv_w, a_log, dt_bias, dn_norm_g, w_proj_a, w_proj_b, w_proj_c, w_out)
```

```python
import numpy as np
import jax
import jax.numpy as jnp
from jax import lax
from jax.experimental import pallas as pl
from jax.experimental.pallas import tpu as pltpu

F32 = jnp.float32
BF16 = jnp.bfloat16

D_MODEL = 1024
ATT_HEADS = 8
ATT_KV_HEADS = 2
ATT_HEAD_DIM = 64
ATT_GROUP = ATT_HEADS // ATT_KV_HEADS
ATT_WIDTH = ATT_HEADS * ATT_HEAD_DIM
ATT_KV_WIDTH = ATT_KV_HEADS * ATT_HEAD_DIM
GROUP_WIDTH = ATT_GROUP * ATT_HEAD_DIM
WINDOW = 128
CONV_WIDTH = 512
CONV_K = 31
DN_HEADS = 4
DN_HEAD_DIM = 128
DN_WIDTH = DN_HEADS * DN_HEAD_DIM
DN_CONV_K = 4
DN_CHUNK = 64
EPS = 1e-6
NEG_INF = -1e30

LANES = 128
SUBLANES = 8

TILE = 256
CONV_PAD = 32
SCONV_PAD = 8
CONV_ROWS = 32
GATE_COLS = 512
VMEM_LIMIT = 56 * 1024 * 1024

OFF_A = 0
OFF_B = OFF_A + 2 * ATT_WIDTH + 2 * ATT_KV_WIDTH
OFF_C = OFF_B + 3 * CONV_WIDTH
MAIN_COLS = OFF_C + 3 * DN_WIDTH
AB_COLS = 2 * DN_HEADS
TAIL_ZC = 0
TAIL_MG = TAIL_ZC + DN_WIDTH
TAIL_COLS = TAIL_MG + 3 * D_MODEL

(V_NORM_G, V_Q_GAIN, V_K_GAIN, V_DW_B, V_LN_G, V_LN_B, V_PW2_B, V_A_LOG, V_DT_BIAS, V_DN_GAIN,
 N_VECS) = range(11)


def _dot(a, b, precision=None):
    return jnp.dot(a, b, preferred_element_type=F32, precision=precision)


def _dot_nt(a, b):
    return lax.dot_general(a, b, (((1,), (1,)), ((), ())), preferred_element_type=F32)


def _silu(x):
    return x * jax.nn.sigmoid(x)


def _softplus(x):
    return jnp.maximum(x, 0.0) + jnp.log1p(jnp.exp(-jnp.abs(x)))


def _ada_body(c_ref, w_ref, b_ref, o_ref):
    sc = _silu(c_ref[...]).astype(BF16)
    o_ref[...] = _dot(sc, w_ref[...].astype(BF16)) + b_ref[...]


def _ada_call(c_pad, w_ada, b_ada):
    depth, d, d3 = w_ada.shape
    nblk = d3 // d
    return pl.pallas_call(
        _ada_body,
        grid=(depth, nblk),
        in_specs=[
            pl.BlockSpec((SUBLANES, d), lambda l, j: (0, 0)),
            pl.BlockSpec((None, d, d), lambda l, j: (l, 0, j)),
            pl.BlockSpec((None, 1, d), lambda l, j: (l, 0, j)),
        ],
        out_specs=pl.BlockSpec((None, SUBLANES, d), lambda l, j: (l, 0, j)),
        out_shape=jax.ShapeDtypeStruct((depth, SUBLANES, d3), F32),
        name="adaln_mod",
    )(c_pad, w_ada, b_ada.reshape(depth, 1, d3))


def _layer_body(x_ref, mod_ref, vec_ref, wmain_ref, wtail_ref, wab_ref, sink_ref, bias_ref,
                dww_ref, pw2w_ref, scw_ref, wpa_ref, wpb_ref, wpc_ref, wout_ref,
                o_ref,
                kprev_ref, vprev_ref, convbuf_ref, sconvbuf_ref, state_ref,
                ya_ref, phase_ref, cv_ref, qkv_ref, yc_ref, gates_ref):
    tm = x_ref.shape[0]
    t = pl.program_id(1)

    @pl.when(t == 0)
    def _reset():
        kprev_ref[...] = jnp.zeros_like(kprev_ref)
        vprev_ref[...] = jnp.zeros_like(vprev_ref)
        convbuf_ref[0:CONV_PAD, :] = jnp.zeros((CONV_PAD, CONV_WIDTH), F32)
        sconvbuf_ref[0:SCONV_PAD, :] = jnp.zeros((SCONV_PAD, 3 * DN_WIDTH), F32)
        state_ref[...] = jnp.zeros_like(state_ref)

    def vec(row, width):
        return vec_ref[row, :, 0:width]

    mod = mod_ref[pl.ds(pl.program_id(0), 1), :]
    shift = mod[:, 0:D_MODEL]
    scale = mod[:, D_MODEL:2 * D_MODEL]
    gate = mod[:, 2 * D_MODEL:3 * D_MODEL]
    x = x_ref[...]
    ms = jnp.mean(x * x, axis=-1, keepdims=True)
    h = x * lax.rsqrt(ms + EPS) * vec(V_NORM_G, D_MODEL)
    h = h * (1.0 + scale) + shift
    hb = h.astype(BF16)

    def proj(w_ref, lo, hi):
        return _dot(hb, w_ref[:, lo:hi])

    pa = proj(wmain_ref, OFF_A, OFF_B)
    qa = pa[:, 0:ATT_WIDTH]
    ka = pa[:, ATT_WIDTH:ATT_WIDTH + ATT_KV_WIDTH]
    va = pa[:, ATT_WIDTH + ATT_KV_WIDTH:ATT_WIDTH + 2 * ATT_KV_WIDTH]
    za = pa[:, ATT_WIDTH + 2 * ATT_KV_WIDTH:]

    def head_mean_square(v, width):
        r = lax.broadcasted_iota(jnp.int32, (width, width), 0) // ATT_HEAD_DIM
        c = lax.broadcasted_iota(jnp.int32, (width, width), 1) // ATT_HEAD_DIM
        avg = jnp.where(r == c, 1.0 / ATT_HEAD_DIM, 0.0).astype(BF16)
        sq = v * v
        hi = sq.astype(BF16)
        lo = (sq - hi.astype(F32)).astype(BF16)
        return _dot(hi, avg) + _dot(lo, avg)

    qn = (qa * lax.rsqrt(head_mean_square(qa, ATT_WIDTH) + EPS) * vec(V_Q_GAIN, ATT_WIDTH)
          * (ATT_HEAD_DIM ** -0.5))
    kn = ka * lax.rsqrt(head_mean_square(ka, ATT_KV_WIDTH) + EPS) * vec(V_K_GAIN, ATT_KV_WIDTH)

    low_half = lax.broadcasted_iota(jnp.int32, (tm, ATT_KV_WIDTH), 1) < ATT_HEAD_DIM
    k_swapped = pltpu.roll(kn, ATT_HEAD_DIM, axis=1)
    krep = [jnp.concatenate([kh, kh], axis=1).astype(BF16)
            for kh in (jnp.where(low_half, kn, k_swapped), jnp.where(low_half, k_swapped, kn))]
    v_t = va.T.astype(BF16)

    q_head = lax.broadcasted_iota(jnp.int32, (WINDOW, GROUP_WIDTH), 1) // ATT_HEAD_DIM
    key_row = lax.broadcasted_iota(jnp.int32, (2 * WINDOW, ATT_GROUP * WINDOW), 0)
    n_masked_rows = jnp.where(t == 0, WINDOW, 0)

    blocks = [(j, g) for j in range(tm // WINDOW) for g in range(ATT_KV_HEADS)]
    scores, v_cats = {}, {}
    for j, g in blocks:
        rows = slice(j * WINDOW, (j + 1) * WINDOW)
        if j == 0:
            k_prev, v_prev = kprev_ref[g].astype(BF16), vprev_ref[...]
        else:
            prev_rows = slice((j - 1) * WINDOW, j * WINDOW)
            k_prev, v_prev = krep[g][prev_rows], v_t[:, prev_rows]
        kcat = jnp.concatenate([k_prev, krep[g][rows]], axis=0)
        v_cats[j] = jnp.concatenate([v_prev, v_t[:, rows]], axis=1)
        qg = qn[rows, g * GROUP_WIDTH:(g + 1) * GROUP_WIDTH]
        qstack = jnp.concatenate(
            [jnp.where(q_head == i, qg, 0.0) for i in range(ATT_GROUP)], axis=0).astype(BF16)
        s = _dot_nt(kcat, qstack) + bias_ref[g]
        if j == 0:
            s = jnp.where(key_row < n_masked_rows, NEG_INF, s)
        scores[j, g] = s
    probs = {}
    for j, g in blocks:
        s = scores[j, g]
        sink = sink_ref[g]
        m = jnp.maximum(jnp.max(s, axis=0, keepdims=True), sink)
        p = jnp.exp(s - m)
        denom = jnp.sum(p, axis=0, keepdims=True) + jnp.exp(sink - m)
        probs[j, g] = (p * (1.0 / denom)).astype(BF16)
    for j, g in blocks:
        o_t = _dot(v_cats[j][g * ATT_HEAD_DIM:(g + 1) * ATT_HEAD_DIM], probs[j, g])
        stacked = jnp.concatenate([o_t[:, i * WINDOW:(i + 1) * WINDOW] for i in range(ATT_GROUP)], axis=0)
        ya_ref[j * WINDOW:(j + 1) * WINDOW, g * GROUP_WIDTH:(g + 1) * GROUP_WIDTH] = stacked.T

    for g in range(ATT_KV_HEADS):
        kprev_ref[g] = krep[g][tm - WINDOW:tm].astype(F32)
    vprev_ref[...] = v_t[:, tm - WINDOW:tm]

    ya = ya_ref[...] * _silu(za)

    pb = proj(wmain_ref, OFF_B, OFF_C)
    ub = pb[:, 0:CONV_WIDTH] * jax.nn.sigmoid(pb[:, CONV_WIDTH:2 * CONV_WIDTH])
    zb = pb[:, 2 * CONV_WIDTH:]
    convbuf_ref[CONV_PAD:CONV_PAD + tm, :] = ub
    base = CONV_PAD - (CONV_K - 1)
    phase_rows = phase_ref.shape[1]
    for ph in range(1, SUBLANES):
        phase_ref[ph - 1] = convbuf_ref[ph:ph + phase_rows, :]
    sconvbuf_ref[SCONV_PAD:SCONV_PAD + tm, :] = proj(wmain_ref, OFF_C, MAIN_COLS)
    for piece in range(3 * D_MODEL // GATE_COLS):
        cols = slice(piece * GATE_COLS, (piece + 1) * GATE_COLS)
        gates_ref[:, cols] = jax.nn.sigmoid(
            proj(wtail_ref, TAIL_MG + piece * GATE_COLS, TAIL_MG + (piece + 1) * GATE_COLS))
    dw_b = vec(V_DW_B, CONV_WIDTH)
    for r in range(tm // CONV_ROWS):
        acc = jnp.broadcast_to(dw_b, (CONV_ROWS, CONV_WIDTH))
        for k in range(CONV_K):
            ph = (base + k) % SUBLANES
            start = (base + k) - ph + r * CONV_ROWS
            if ph == 0:
                win = convbuf_ref[start:start + CONV_ROWS, :]
            else:
                win = phase_ref[ph - 1, start:start + CONV_ROWS, :]
            acc = acc + jnp.concatenate([dww_ref[k]] * (CONV_ROWS // SUBLANES), axis=0) * win
        cv_ref[r * CONV_ROWS:(r + 1) * CONV_ROWS, :] = acc
    convbuf_ref[0:CONV_PAD, :] = convbuf_ref[tm:tm + CONV_PAD, :]
    cv = cv_ref[...]
    mu = jnp.mean(cv, axis=-1, keepdims=True)
    xc = cv - mu
    var = jnp.mean(xc * xc, axis=-1, keepdims=True)
    ln = xc * lax.rsqrt(var + EPS) * vec(V_LN_G, CONV_WIDTH) + vec(V_LN_B, CONV_WIDTH)
    yb = (_dot(_silu(ln).astype(BF16), pw2w_ref[...]) + vec(V_PW2_B, CONV_WIDTH)) * _silu(zb)
    merged = _dot(ya.astype(BF16), wpa_ref[...]) * gates_ref[:, 0:D_MODEL]
    merged = merged + _dot(yb.astype(BF16), wpb_ref[...]) * gates_ref[:, D_MODEL:2 * D_MODEL]

    sbase = SCONV_PAD - (DN_CONV_K - 1)
    for r in range(tm // CONV_ROWS):
        for cblk in range(3):
            cols = slice(cblk * DN_WIDTH, (cblk + 1) * DN_WIDTH)
            acc = None
            for k in range(DN_CONV_K):
                start = sbase + k + r * CONV_ROWS
                tap = jnp.concatenate([scw_ref[k, :, cols]] * (CONV_ROWS // SUBLANES), axis=0)
                term = tap * sconvbuf_ref[start:start + CONV_ROWS, cols]
                acc = term if acc is None else acc + term
            qkv_ref[r * CONV_ROWS:(r + 1) * CONV_ROWS, cols] = _silu(acc)
    sconvbuf_ref[0:SCONV_PAD, :] = sconvbuf_ref[tm:tm + SCONV_PAD, :]

    ab = _dot(hb, wab_ref[...])
    g_col = -jnp.exp(vec(V_A_LOG, LANES)) * _softplus(ab + vec(V_DT_BIAS, LANES))
    beta_col = jax.nn.sigmoid(ab)

    n_chunks = tm // DN_CHUNK
    ri = lax.broadcasted_iota(jnp.int32, (tm, tm), 0)
    ci = lax.broadcasted_iota(jnp.int32, (tm, tm), 1)
    same_chunk = (ri // DN_CHUNK) == (ci // DN_CHUNK)
    hp = lax.Precision.HIGHEST
    gc_col = _dot(jnp.where(same_chunk & (ri >= ci), 1.0, 0.0), g_col, precision=hp)
    gl_col = _dot(jnp.where(same_chunk, 1.0, 0.0), g_col, precision=hp)
    gc_row = gc_col.T

    heads = range(DN_HEADS)
    chunks = range(n_chunks)
    crows = [slice(c * DN_CHUNK, (c + 1) * DN_CHUNK) for c in chunks]
    cat_w = n_chunks * DN_CHUNK
    cat_row = lax.broadcasted_iota(jnp.int32, (DN_CHUNK, cat_w), 0)
    cat_lane = lax.broadcasted_iota(jnp.int32, (DN_CHUNK, cat_w), 1)
    cat_blk = cat_lane // DN_CHUNK
    cat_col = cat_lane - cat_blk * DN_CHUNK
    lower = cat_row >= cat_col
    strict = cat_row > cat_col
    diag_blk = (lax.broadcasted_iota(jnp.int32, (cat_w, cat_w), 0) // DN_CHUNK
                == lax.broadcasted_iota(jnp.int32, (cat_w, cat_w), 1) // DN_CHUNK)
    key_blk = (lax.broadcasted_iota(jnp.int32, (tm, n_chunks * DN_HEAD_DIM), 0) // DN_CHUNK
               == lax.broadcasted_iota(jnp.int32, (tm, n_chunks * DN_HEAD_DIM), 1) // DN_HEAD_DIM)

    def block_diag(m_cat):
        tiled = jnp.concatenate([m_cat] * n_chunks, axis=0)
        return jnp.where(diag_blk, tiled, jnp.zeros_like(tiled))

    def chunk_cat(v):
        return jnp.concatenate([v[crows[c]] for c in chunks], axis=1)

    qe_l, ke_l, xin_l, gl_l, sc_l, decay_l = [], [], [], [], [], []
    for hd in heads:
        qh = qkv_ref[:, hd * DN_HEAD_DIM:(hd + 1) * DN_HEAD_DIM]
        kh = qkv_ref[:, DN_WIDTH + hd * DN_HEAD_DIM:DN_WIDTH + (hd + 1) * DN_HEAD_DIM]
        vh = qkv_ref[:, 2 * DN_WIDTH + hd * DN_HEAD_DIM:2 * DN_WIDTH + (hd + 1) * DN_HEAD_DIM]
        qh = qh * lax.rsqrt(jnp.sum(qh * qh, axis=-1, keepdims=True) + EPS) * (DN_HEAD_DIM ** -0.5)
        kh = kh * lax.rsqrt(jnp.sum(kh * kh, axis=-1, keepdims=True) + EPS)
        beta = beta_col[:, DN_HEADS + hd:DN_HEADS + hd + 1]
        gc = gc_col[:, hd:hd + 1]
        gl = gl_col[:, hd:hd + 1]
        kb = kh * beta
        eg = jnp.exp(gc)
        qe_l.append(qh * eg)
        ke_l.append(kh * jnp.exp(gl - gc))
        xin_l.append(jnp.concatenate([vh * beta, kb * eg], axis=1))
        gl_l.append(gl)
        kq = jnp.concatenate([chunk_cat(kb), chunk_cat(qh)], axis=0).astype(BF16)
        k_tiled = jnp.concatenate([kh.astype(BF16)] * n_chunks, axis=1)
        k_bd = jnp.where(key_blk, k_tiled, jnp.zeros_like(k_tiled))
        sc_l.append(_dot_nt(kq, k_bd))
        gc_i = gc[crows[n_chunks - 1]]
        for c in reversed(range(n_chunks - 1)):
            gc_i = jnp.where(cat_blk == c, gc[crows[c]], gc_i)
        decay_l.append(jnp.exp(jnp.where(lower, gc_i - gc_row[hd:hd + 1, :], NEG_INF)))

    pw_l = [jnp.where(strict, sc_l[hd][0:DN_CHUNK] * decay_l[hd], 0.0) for hd in heads]
    intra_l = [(sc_l[hd][DN_CHUNK:] * decay_l[hd]).astype(BF16) for hd in heads]
    q_l = [-pw_l[hd] for hd in heads]
    n_steps = int(np.log2(DN_CHUNK))
    for step in range(n_steps):
        bd_l = [block_diag(pw_l[hd].astype(BF16)) for hd in heads]
        if step == 0:
            pw_l = [_dot(pw_l[hd].astype(BF16), bd_l[hd]) for hd in heads]
            continue
        last = step == n_steps - 1
        lhs_l = [q_l[hd].astype(BF16) if last else
                 jnp.concatenate([pw_l[hd], q_l[hd]], axis=0).astype(BF16) for hd in heads]
        r_l = [_dot(lhs_l[hd], bd_l[hd]) for hd in heads]
        q_l = [q_l[hd] + pw_l[hd] + r_l[hd][-DN_CHUNK:] for hd in heads]
        if not last:
            pw_l = [r_l[hd][0:DN_CHUNK] for hd in heads]
    uw_l = [xin_l[hd] + _dot(block_diag(q_l[hd].astype(BF16)), xin_l[hd].astype(BF16)) for hd in heads]

    state = [state_ref[hd] for hd in heads]
    o_inter = [[] for _ in heads]
    v_news = [[] for _ in heads]
    for c in chunks:
        res = []
        for hd in heads:
            wq = jnp.concatenate([uw_l[hd][crows[c], DN_HEAD_DIM:], qe_l[hd][crows[c]]], axis=0).astype(BF16)
            res.append(_dot(wq, state[hd].astype(BF16)))
        for hd in heads:
            v_new = uw_l[hd][crows[c], 0:DN_HEAD_DIM] - res[hd][0:DN_CHUNK]
            v_news[hd].append(v_new)
            o_inter[hd].append(res[hd][DN_CHUNK:])
            g_last = gl_l[hd][c * DN_CHUNK:c * DN_CHUNK + 1, :]
            state[hd] = state[hd] * jnp.exp(g_last) + _dot(ke_l[hd][crows[c]].T.astype(BF16),
                                                           v_new.astype(BF16))
    dn_gain = vec(V_DN_GAIN, DN_HEAD_DIM)
    for hd in heads:
        state_ref[hd] = state[hd]
        v_new_all = jnp.concatenate(v_news[hd], axis=0).astype(BF16)
        od = jnp.concatenate(o_inter[hd], axis=0) + _dot(block_diag(intra_l[hd]), v_new_all)
        od = od * lax.rsqrt(jnp.mean(od * od, axis=-1, keepdims=True) + EPS) * dn_gain
        yc_ref[:, hd * DN_HEAD_DIM:(hd + 1) * DN_HEAD_DIM] = od

    yc = yc_ref[...] * _silu(proj(wtail_ref, TAIL_ZC, TAIL_MG))
    merged = merged + _dot(yc.astype(BF16), wpc_ref[...]) * gates_ref[:, 2 * D_MODEL:3 * D_MODEL]

    o_ref[...] = x + gate * _dot(merged.astype(BF16), wout_ref[...])


def _alibi_bias():
    qi = np.arange(WINDOW)[:, None]
    kj = np.arange(2 * WINDOW)[None, :]
    dist = qi + WINDOW - kj
    valid = (dist >= 0) & (dist < WINDOW)
    slopes = np.exp2(-8.0 * np.arange(1, ATT_HEADS + 1, dtype=np.float32) / ATT_HEADS)
    bias = np.where(valid[None], -slopes[:, None, None] * dist[None].astype(np.float32), NEG_INF)
    bias = bias.reshape(ATT_KV_HEADS, ATT_GROUP * WINDOW, 2 * WINDOW)
    return jnp.asarray(np.swapaxes(bias, 1, 2), dtype=F32)


def _layer_call(layer, x, mod, p):
    bsz, seq, d = x.shape
    tm = TILE
    assert seq % tm == 0 and d == D_MODEL and bsz <= SUBLANES

    def layer_spec(a):
        nd = a.ndim - 1
        return pl.BlockSpec((None,) + a.shape[1:], lambda b, t: (layer,) + (0,) * nd,
                            pipeline_mode=pl.Buffered(1))

    def const_spec(a):
        nd = a.ndim
        return pl.BlockSpec(a.shape, lambda b, t: (0,) * nd, pipeline_mode=pl.Buffered(1))

    tile_spec = pl.BlockSpec((None, tm, d), lambda b, t: (b, t, 0))
    per_layer = [mod, p["vecs"], p["w_main"], p["w_tail"], p["w_ab"], p["sinks"]]
    after_bias = [p["dw_w"], p["pw2_w"], p["sconv_w"], p["w_proj_a"], p["w_proj_b"], p["w_proj_c"],
                  p["w_out"]]
    return pl.pallas_call(
        _layer_body,
        grid=(bsz, seq // tm),
        in_specs=([tile_spec] + [layer_spec(a) for a in per_layer] + [const_spec(p["bias"])]
                  + [layer_spec(a) for a in after_bias]),
        out_specs=tile_spec,
        out_shape=jax.ShapeDtypeStruct(x.shape, x.dtype),
        scratch_shapes=[
            pltpu.VMEM((ATT_KV_HEADS, WINDOW, GROUP_WIDTH), F32),
            pltpu.VMEM((ATT_KV_WIDTH, WINDOW), BF16),
            pltpu.VMEM((CONV_PAD + tm, CONV_WIDTH), F32),
            pltpu.VMEM((SCONV_PAD + tm, 3 * DN_WIDTH), F32),
            pltpu.VMEM((DN_HEADS, DN_HEAD_DIM, DN_HEAD_DIM), F32),
            pltpu.VMEM((tm, ATT_WIDTH), F32),
            pltpu.VMEM((SUBLANES - 1, tm + CONV_PAD - SUBLANES, CONV_WIDTH), F32),
            pltpu.VMEM((tm, CONV_WIDTH), F32),
            pltpu.VMEM((tm, 3 * DN_WIDTH), F32),
            pltpu.VMEM((tm, DN_WIDTH), F32),
            pltpu.VMEM((tm, 3 * D_MODEL), F32),
        ],
        compiler_params=pltpu.CompilerParams(
            dimension_semantics=("arbitrary", "arbitrary"),
            vmem_limit_bytes=VMEM_LIMIT),
        name="hybrid_layer",
    )(x, *per_layer, p["bias"], *after_bias)


def _prep_params(norm_g, w_in, q_norm_g, k_norm_g, sinks, dw_w, dw_b, ln_g, ln_b, pw2_w, pw2_b,
                 sconv_w, a_log, dt_bias, dn_norm_g, w_proj_a, w_proj_b, w_proj_c, w_out):
    depth, d, _ = w_in.shape

    def row(v):
        return jnp.pad(v, ((0, 0), (0, D_MODEL - v.shape[1])))[:, None, :]

    vec_rows = [None] * N_VECS
    vec_rows[V_NORM_G] = row(norm_g)
    vec_rows[V_Q_GAIN] = row(jnp.tile(q_norm_g, (1, ATT_HEADS)))
    vec_rows[V_K_GAIN] = row(jnp.tile(k_norm_g, (1, ATT_KV_HEADS)))
    vec_rows[V_DW_B] = row(dw_b)
    vec_rows[V_LN_G] = row(ln_g)
    vec_rows[V_LN_B] = row(ln_b)
    vec_rows[V_PW2_B] = row(pw2_b)
    vec_rows[V_A_LOG] = row(a_log)
    vec_rows[V_DT_BIAS] = row(dt_bias)
    vec_rows[V_DN_GAIN] = row(dn_norm_g)
    ab_hi = MAIN_COLS + AB_COLS
    return dict(
        vecs=jnp.stack(vec_rows, axis=1),
        w_main=w_in[:, :, :MAIN_COLS].astype(BF16),
        w_tail=w_in[:, :, ab_hi:].astype(BF16),
        w_ab=jnp.pad(w_in[:, :, MAIN_COLS:ab_hi], ((0, 0), (0, 0), (0, LANES - AB_COLS))).astype(BF16),
        sinks=jnp.repeat(sinks, WINDOW, axis=1).reshape(depth, ATT_KV_HEADS, 1, ATT_GROUP * WINDOW),
        bias=_alibi_bias(),
        dw_w=jnp.broadcast_to(dw_w[:, :, None, :], (depth, CONV_K, SUBLANES, CONV_WIDTH)),
        pw2_w=pw2_w.astype(BF16),
        sconv_w=jnp.broadcast_to(sconv_w[:, :, None, :], (depth, DN_CONV_K, SUBLANES, 3 * DN_WIDTH)),
        w_proj_a=w_proj_a.astype(BF16),
        w_proj_b=w_proj_b.astype(BF16),
        w_proj_c=w_proj_c.astype(BF16),
        w_out=w_out.astype(BF16),
    )


@jax.jit
def _forward(x, c, w_ada, b_ada, norm_g, w_in, q_norm_g, k_norm_g, sinks, dw_w, dw_b, ln_g, ln_b,
             pw2_w, pw2_b, sconv_w, a_log, dt_bias, dn_norm_g, w_proj_a, w_proj_b, w_proj_c, w_out):
    bsz, _, d = x.shape
    depth = w_ada.shape[0]
    c_pad = jnp.zeros((SUBLANES, d), F32).at[:bsz].set(c)
    mod = _ada_call(c_pad, w_ada, b_ada)
    p = _prep_params(norm_g, w_in, q_norm_g, k_norm_g, sinks, dw_w, dw_b, ln_g, ln_b, pw2_w, pw2_b,
                     sconv_w, a_log, dt_bias, dn_norm_g, w_proj_a, w_proj_b, w_proj_c, w_out)
    for layer in range(depth):
        x = _layer_call(layer, x, mod, p)
    return x


def kernel(x, c, w_ada, b_ada, norm_g, w_in, q_norm_g, k_norm_g, sinks, dw_w, dw_b, ln_g, ln_b, pw2_w, pw2_b, sconv_w, a_log, dt_bias, dn_norm_g, w_proj_a, w_proj_b, w_proj_c, w_out):
    return _forward(x, c, w_ada, b_ada, norm_g, w_in, q_norm_g, k_norm_g, sinks, dw_w, dw_b, ln_g, ln_b,
                    pw2_w, pw2_b, sconv_w, a_log, dt_bias, dn_norm_g, w_proj_a, w_proj_b, w_proj_c, w_out)
```

```python
import numpy as np
import jax
import jax.numpy as jnp
from jax import lax
from jax.experimental import pallas as pl
from jax.experimental.pallas import tpu as pltpu

F32 = jnp.float32
BF16 = jnp.bfloat16

D_MODEL = 1024
ATT_HEADS = 8
ATT_KV_HEADS = 2
ATT_HEAD_DIM = 64
ATT_GROUP = ATT_HEADS // ATT_KV_HEADS
ATT_WIDTH = ATT_HEADS * ATT_HEAD_DIM
ATT_KV_WIDTH = ATT_KV_HEADS * ATT_HEAD_DIM
GROUP_WIDTH = ATT_GROUP * ATT_HEAD_DIM
WINDOW = 128
CONV_WIDTH = 512
CONV_K = 31
DN_HEADS = 4
DN_HEAD_DIM = 128
DN_WIDTH = DN_HEADS * DN_HEAD_DIM
DN_CONV_K = 4
DN_CHUNK = 64
EPS = 1e-6
NEG_INF = -1e30

LANES = 128
SUBLANES = 8

TILE = 256
CONV_PAD = 32
SCONV_PAD = 8
CONV_ROWS = 32
GATE_COLS = 512
VMEM_LIMIT = 56 * 1024 * 1024

OFF_A = 0
OFF_B = OFF_A + 2 * ATT_WIDTH + 2 * ATT_KV_WIDTH
OFF_C = OFF_B + 3 * CONV_WIDTH
MAIN_COLS = OFF_C + 3 * DN_WIDTH
AB_COLS = 2 * DN_HEADS
TAIL_ZC = 0
TAIL_MG = TAIL_ZC + DN_WIDTH
TAIL_COLS = TAIL_MG + 3 * D_MODEL

(V_NORM_G, V_Q_GAIN, V_K_GAIN, V_DW_B, V_LN_G, V_LN_B, V_PW2_B, V_A_LOG, V_DT_BIAS, V_DN_GAIN,
 N_VECS) = range(11)


def _dot(a, b):
    return jnp.dot(a, b, preferred_element_type=F32)


def _dot_nt(a, b):
    return lax.dot_general(a, b, (((1,), (1,)), ((), ())), preferred_element_type=F32)


def _silu(x):
    return x * jax.nn.sigmoid(x)


def _softplus(x):
    return jnp.maximum(x, 0.0) + jnp.log1p(jnp.exp(-jnp.abs(x)))


def _ada_body(c_ref, w_ref, b_ref, o_ref):
    sc = _silu(c_ref[...]).astype(BF16)
    o_ref[...] = _dot(sc, w_ref[...].astype(BF16)) + b_ref[...]


def _ada_call(c_pad, w_ada, b_ada):
    depth, d, d3 = w_ada.shape
    nblk = d3 // d
    return pl.pallas_call(
        _ada_body,
        grid=(depth, nblk),
        in_specs=[
            pl.BlockSpec((SUBLANES, d), lambda l, j: (0, 0)),
            pl.BlockSpec((None, d, d), lambda l, j: (l, 0, j)),
            pl.BlockSpec((None, 1, d), lambda l, j: (l, 0, j)),
        ],
        out_specs=pl.BlockSpec((None, SUBLANES, d), lambda l, j: (l, 0, j)),
        out_shape=jax.ShapeDtypeStruct((depth, SUBLANES, d3), F32),
        name="adaln_mod",
    )(c_pad, w_ada, b_ada.reshape(depth, 1, d3))


def _layer_body(x_ref, mod_ref, vec_ref, wmain_ref, wtail_ref, wab_ref, sink_ref, bias_ref,
                dww_ref, pw2w_ref, scw_ref, wpa_ref, wpb_ref, wpc_ref, wout_ref,
                o_ref,
                kprev_ref, vprev_ref, convbuf_ref, sconvbuf_ref, state_ref,
                ya_ref, phase_ref, cv_ref, qkv_ref, yc_ref, gates_ref):
    tm = x_ref.shape[0]
    t = pl.program_id(1)

    @pl.when(t == 0)
    def _reset():
        kprev_ref[...] = jnp.zeros_like(kprev_ref)
        vprev_ref[...] = jnp.zeros_like(vprev_ref)
        convbuf_ref[0:CONV_PAD, :] = jnp.zeros((CONV_PAD, CONV_WIDTH), F32)
        sconvbuf_ref[0:SCONV_PAD, :] = jnp.zeros((SCONV_PAD, 3 * DN_WIDTH), F32)
        state_ref[...] = jnp.zeros_like(state_ref)

    def vec(row, width):
        return vec_ref[row, :, 0:width]

    mod = mod_ref[pl.ds(pl.program_id(0), 1), :]
    shift = mod[:, 0:D_MODEL]
    scale = mod[:, D_MODEL:2 * D_MODEL]
    gate = mod[:, 2 * D_MODEL:3 * D_MODEL]
    x = x_ref[...]
    ms = jnp.mean(x * x, axis=-1, keepdims=True)
    h = x * lax.rsqrt(ms + EPS) * vec(V_NORM_G, D_MODEL)
    h = h * (1.0 + scale) + shift
    hb = h.astype(BF16)

    def proj(w_ref, lo, hi):
        return _dot(hb, w_ref[:, lo:hi])

    pa = proj(wmain_ref, OFF_A, OFF_B)
    qa = pa[:, 0:ATT_WIDTH]
    ka = pa[:, ATT_WIDTH:ATT_WIDTH + ATT_KV_WIDTH]
    va = pa[:, ATT_WIDTH + ATT_KV_WIDTH:ATT_WIDTH + 2 * ATT_KV_WIDTH]
    za = pa[:, ATT_WIDTH + 2 * ATT_KV_WIDTH:]

    def head_mean_square(v, width):
        r = lax.broadcasted_iota(jnp.int32, (width, width), 0) // ATT_HEAD_DIM
        c = lax.broadcasted_iota(jnp.int32, (width, width), 1) // ATT_HEAD_DIM
        avg = jnp.where(r == c, 1.0 / ATT_HEAD_DIM, 0.0).astype(BF16)
        sq = v * v
        hi = sq.astype(BF16)
        lo = (sq - hi.astype(F32)).astype(BF16)
        return _dot(hi, avg) + _dot(lo, avg)

    qn = (qa * lax.rsqrt(head_mean_square(qa, ATT_WIDTH) + EPS) * vec(V_Q_GAIN, ATT_WIDTH)
          * (ATT_HEAD_DIM ** -0.5))
    kn = ka * lax.rsqrt(head_mean_square(ka, ATT_KV_WIDTH) + EPS) * vec(V_K_GAIN, ATT_KV_WIDTH)

    low_half = lax.broadcasted_iota(jnp.int32, (tm, ATT_KV_WIDTH), 1) < ATT_HEAD_DIM
    k_swapped = pltpu.roll(kn, ATT_HEAD_DIM, axis=1)
    krep = [jnp.concatenate([kh, kh], axis=1).astype(BF16)
            for kh in (jnp.where(low_half, kn, k_swapped), jnp.where(low_half, k_swapped, kn))]
    v_t = va.T.astype(BF16)

    q_head = lax.broadcasted_iota(jnp.int32, (WINDOW, GROUP_WIDTH), 1) // ATT_HEAD_DIM
    key_row = lax.broadcasted_iota(jnp.int32, (2 * WINDOW, ATT_GROUP * WINDOW), 0)
    n_masked_rows = jnp.where(t == 0, WINDOW, 0)

    blocks = [(j, g) for j in range(tm // WINDOW) for g in range(ATT_KV_HEADS)]
    scores, v_cats = {}, {}
    for j, g in blocks:
        rows = slice(j * WINDOW, (j + 1) * WINDOW)
        if j == 0:
            k_prev, v_prev = kprev_ref[g].astype(BF16), vprev_ref[...]
        else:
            prev_rows = slice((j - 1) * WINDOW, j * WINDOW)
            k_prev, v_prev = krep[g][prev_rows], v_t[:, prev_rows]
        kcat = jnp.concatenate([k_prev, krep[g][rows]], axis=0)
        v_cats[j] = jnp.concatenate([v_prev, v_t[:, rows]], axis=1)
        qg = qn[rows, g * GROUP_WIDTH:(g + 1) * GROUP_WIDTH]
        qstack = jnp.concatenate(
            [jnp.where(q_head == i, qg, 0.0) for i in range(ATT_GROUP)], axis=0).astype(BF16)
        s = _dot_nt(kcat, qstack) + bias_ref[g]
        if j == 0:
            s = jnp.where(key_row < n_masked_rows, NEG_INF, s)
        scores[j, g] = s
    probs = {}
    for j, g in blocks:
        s = scores[j, g]
        sink = sink_ref[g]
        m = jnp.maximum(jnp.max(s, axis=0, keepdims=True), sink)
        p = jnp.exp(s - m)
        denom = jnp.sum(p, axis=0, keepdims=True) + jnp.exp(sink - m)
        probs[j, g] = (p * (1.0 / denom)).astype(BF16)
    for j, g in blocks:
        o_t = _dot(v_cats[j][g * ATT_HEAD_DIM:(g + 1) * ATT_HEAD_DIM], probs[j, g])
        stacked = jnp.concatenate([o_t[:, i * WINDOW:(i + 1) * WINDOW] for i in range(ATT_GROUP)], axis=0)
        ya_ref[j * WINDOW:(j + 1) * WINDOW, g * GROUP_WIDTH:(g + 1) * GROUP_WIDTH] = stacked.T

    for g in range(ATT_KV_HEADS):
        kprev_ref[g] = krep[g][tm - WINDOW:tm].astype(F32)
    vprev_ref[...] = v_t[:, tm - WINDOW:tm]

    ya = ya_ref[...] * _silu(za)

    pb = proj(wmain_ref, OFF_B, OFF_C)
    ub = pb[:, 0:CONV_WIDTH] * jax.nn.sigmoid(pb[:, CONV_WIDTH:2 * CONV_WIDTH])
    zb = pb[:, 2 * CONV_WIDTH:]
    convbuf_ref[CONV_PAD:CONV_PAD + tm, :] = ub
    base = CONV_PAD - (CONV_K - 1)
    phase_rows = phase_ref.shape[1]
    for ph in range(1, SUBLANES):
        phase_ref[ph - 1] = convbuf_ref[ph:ph + phase_rows, :]
    sconvbuf_ref[SCONV_PAD:SCONV_PAD + tm, :] = proj(wmain_ref, OFF_C, MAIN_COLS)
    for piece in range(3 * D_MODEL // GATE_COLS):
        cols = slice(piece * GATE_COLS, (piece + 1) * GATE_COLS)
        gates_ref[:, cols] = jax.nn.sigmoid(
            proj(wtail_ref, TAIL_MG + piece * GATE_COLS, TAIL_MG + (piece + 1) * GATE_COLS))
    dw_b = vec(V_DW_B, CONV_WIDTH)
    for r in range(tm // CONV_ROWS):
        acc = jnp.broadcast_to(dw_b, (CONV_ROWS, CONV_WIDTH))
        for k in range(CONV_K):
            ph = (base + k) % SUBLANES
            start = (base + k) - ph + r * CONV_ROWS
            if ph == 0:
                win = convbuf_ref[start:start + CONV_ROWS, :]
            else:
                win = phase_ref[ph - 1, start:start + CONV_ROWS, :]
            acc = acc + jnp.concatenate([dww_ref[k]] * (CONV_ROWS // SUBLANES), axis=0) * win
        cv_ref[r * CONV_ROWS:(r + 1) * CONV_ROWS, :] = acc
    convbuf_ref[0:CONV_PAD, :] = convbuf_ref[tm:tm + CONV_PAD, :]
    cv = cv_ref[...]
    mu = jnp.mean(cv, axis=-1, keepdims=True)
    xc = cv - mu
    var = jnp.mean(xc * xc, axis=-1, keepdims=True)
    ln = xc * lax.rsqrt(var + EPS) * vec(V_LN_G, CONV_WIDTH) + vec(V_LN_B, CONV_WIDTH)
    yb = (_dot(_silu(ln).astype(BF16), pw2w_ref[...]) + vec(V_PW2_B, CONV_WIDTH)) * _silu(zb)
    merged = _dot(ya.astype(BF16), wpa_ref[...]) * gates_ref[:, 0:D_MODEL]
    merged = merged + _dot(yb.astype(BF16), wpb_ref[...]) * gates_ref[:, D_MODEL:2 * D_MODEL]

    sbase = SCONV_PAD - (DN_CONV_K - 1)
    for r in range(tm // CONV_ROWS):
        for cblk in range(3):
            cols = slice(cblk * DN_WIDTH, (cblk + 1) * DN_WIDTH)
            acc = None
            for k in range(DN_CONV_K):
                start = sbase + k + r * CONV_ROWS
                tap = jnp.concatenate([scw_ref[k, :, cols]] * (CONV_ROWS // SUBLANES), axis=0)
                term = tap * sconvbuf_ref[start:start + CONV_ROWS, cols]
                acc = term if acc is None else acc + term
            qkv_ref[r * CONV_ROWS:(r + 1) * CONV_ROWS, cols] = _silu(acc)
    sconvbuf_ref[0:SCONV_PAD, :] = sconvbuf_ref[tm:tm + SCONV_PAD, :]

    ab = _dot(hb, wab_ref[...])
    g_col = -jnp.exp(vec(V_A_LOG, LANES)) * _softplus(ab + vec(V_DT_BIAS, LANES))
    beta_col = jax.nn.sigmoid(ab)

    n_chunks = tm // DN_CHUNK
    ri = lax.broadcasted_iota(jnp.int32, (tm, tm), 0)
    ci = lax.broadcasted_iota(jnp.int32, (tm, tm), 1)
    same_chunk = (ri // DN_CHUNK) == (ci // DN_CHUNK)
    g_hi = g_col.astype(BF16)
    g_mid = (g_col - g_hi.astype(F32)).astype(BF16)
    g_lo = (g_col - g_hi.astype(F32) - g_mid.astype(F32)).astype(BF16)
    tri = jnp.where(same_chunk & (ri >= ci), 1.0, 0.0).astype(BF16)
    ones = jnp.where(same_chunk, 1.0, 0.0).astype(BF16)
    gc_col = _dot(tri, g_hi) + _dot(tri, g_mid) + _dot(tri, g_lo)
    gl_col = _dot(ones, g_hi) + _dot(ones, g_mid) + _dot(ones, g_lo)
    gc_row = gc_col.T

    heads = range(DN_HEADS)
    chunks = range(n_chunks)
    crows = [slice(c * DN_CHUNK, (c + 1) * DN_CHUNK) for c in chunks]
    cat_w = n_chunks * DN_CHUNK
    cat_row = lax.broadcasted_iota(jnp.int32, (DN_CHUNK, cat_w), 0)
    cat_lane = lax.broadcasted_iota(jnp.int32, (DN_CHUNK, cat_w), 1)
    cat_blk = cat_lane // DN_CHUNK
    cat_col = cat_lane - cat_blk * DN_CHUNK
    lower = cat_row >= cat_col
    strict = cat_row > cat_col
    diag_blk = (lax.broadcasted_iota(jnp.int32, (cat_w, cat_w), 0) // DN_CHUNK
                == lax.broadcasted_iota(jnp.int32, (cat_w, cat_w), 1) // DN_CHUNK)
    key_blk = (lax.broadcasted_iota(jnp.int32, (tm, n_chunks * DN_HEAD_DIM), 0) // DN_CHUNK
               == lax.broadcasted_iota(jnp.int32, (tm, n_chunks * DN_HEAD_DIM), 1) // DN_HEAD_DIM)

    def block_diag(m_cat):
        tiled = jnp.concatenate([m_cat] * n_chunks, axis=0)
        return jnp.where(diag_blk, tiled, jnp.zeros_like(tiled))

    def chunk_cat(v):
        return jnp.concatenate([v[crows[c]] for c in chunks], axis=1)

    qe_l, ke_l, xin_l, gl_l, sc_l, decay_l = [], [], [], [], [], []
    for hd in heads:
        qh = qkv_ref[:, hd * DN_HEAD_DIM:(hd + 1) * DN_HEAD_DIM]
        kh = qkv_ref[:, DN_WIDTH + hd * DN_HEAD_DIM:DN_WIDTH + (hd + 1) * DN_HEAD_DIM]
        vh = qkv_ref[:, 2 * DN_WIDTH + hd * DN_HEAD_DIM:2 * DN_WIDTH + (hd + 1) * DN_HEAD_DIM]
        qh = qh * lax.rsqrt(jnp.sum(qh * qh, axis=-1, keepdims=True) + EPS) * (DN_HEAD_DIM ** -0.5)
        kh = kh * lax.rsqrt(jnp.sum(kh * kh, axis=-1, keepdims=True) + EPS)
        beta = beta_col[:, DN_HEADS + hd:DN_HEADS + hd + 1]
        gc = gc_col[:, hd:hd + 1]
        gl = gl_col[:, hd:hd + 1]
        kb = kh * beta
        eg = jnp.exp(gc)
        qe_l.append(qh * eg)
        ke_l.append(kh * jnp.exp(gl - gc))
        xin_l.append(jnp.concatenate([vh * beta, kb * eg], axis=1))
        gl_l.append(gl)
        kq = jnp.concatenate([chunk_cat(kb), chunk_cat(qh)], axis=0).astype(BF16)
        k_tiled = jnp.concatenate([kh.astype(BF16)] * n_chunks, axis=1)
        k_bd = jnp.where(key_blk, k_tiled, jnp.zeros_like(k_tiled))
        sc_l.append(_dot_nt(kq, k_bd))
        gc_i = gc[crows[n_chunks - 1]]
        for c in reversed(range(n_chunks - 1)):
            gc_i = jnp.where(cat_blk == c, gc[crows[c]], gc_i)
        decay_l.append(jnp.exp(jnp.where(lower, gc_i - gc_row[hd:hd + 1, :], NEG_INF)))

    pw_l = [jnp.where(strict, sc_l[hd][0:DN_CHUNK] * decay_l[hd], 0.0) for hd in heads]
    intra_l = [(sc_l[hd][DN_CHUNK:] * decay_l[hd]).astype(BF16) for hd in heads]
    q_l = [-pw_l[hd] for hd in heads]
    n_steps = int(np.log2(DN_CHUNK))
    for step in range(n_steps):
        bd_l = [block_diag(pw_l[hd].astype(BF16)) for hd in heads]
        if step == 0:
            pw_l = [_dot(pw_l[hd].astype(BF16), bd_l[hd]) for hd in heads]
            continue
        last = step == n_steps - 1
        lhs_l = [q_l[hd].astype(BF16) if last else
                 jnp.concatenate([pw_l[hd], q_l[hd]], axis=0).astype(BF16) for hd in heads]
        r_l = [_dot(lhs_l[hd], bd_l[hd]) for hd in heads]
        q_l = [q_l[hd] + pw_l[hd] + r_l[hd][-DN_CHUNK:] for hd in heads]
        if not last:
            pw_l = [r_l[hd][0:DN_CHUNK] for hd in heads]
    uw_l = [xin_l[hd] + _dot(block_diag(q_l[hd].astype(BF16)), xin_l[hd].astype(BF16)) for hd in heads]

    state = [state_ref[hd] for hd in heads]
    o_inter = [[] for _ in heads]
    v_news = [[] for _ in heads]
    for c in chunks:
        res = []
        for hd in heads:
            wq = jnp.concatenate([uw_l[hd][crows[c], DN_HEAD_DIM:], qe_l[hd][crows[c]]], axis=0).astype(BF16)
            res.append(_dot(wq, state[hd].astype(BF16)))
        for hd in heads:
            v_new = uw_l[hd][crows[c], 0:DN_HEAD_DIM] - res[hd][0:DN_CHUNK]
            v_news[hd].append(v_new)
            o_inter[hd].append(res[hd][DN_CHUNK:])
            g_last = gl_l[hd][c * DN_CHUNK:c * DN_CHUNK + 1, :]
            state[hd] = state[hd] * jnp.exp(g_last) + _dot(ke_l[hd][crows[c]].T.astype(BF16),
                                                           v_new.astype(BF16))
    dn_gain = vec(V_DN_GAIN, DN_HEAD_DIM)
    for hd in heads:
        state_ref[hd] = state[hd]
        v_new_all = jnp.concatenate(v_news[hd], axis=0).astype(BF16)
        od = jnp.concatenate(o_inter[hd], axis=0) + _dot(block_diag(intra_l[hd]), v_new_all)
        od = od * lax.rsqrt(jnp.mean(od * od, axis=-1, keepdims=True) + EPS) * dn_gain
        yc_ref[:, hd * DN_HEAD_DIM:(hd + 1) * DN_HEAD_DIM] = od

    yc = yc_ref[...] * _silu(proj(wtail_ref, TAIL_ZC, TAIL_MG))
    merged = merged + _dot(yc.astype(BF16), wpc_ref[...]) * gates_ref[:, 2 * D_MODEL:3 * D_MODEL]

    o_ref[...] = x + gate * _dot(merged.astype(BF16), wout_ref[...])


def _alibi_bias():
    qi = np.arange(WINDOW)[:, None]
    kj = np.arange(2 * WINDOW)[None, :]
    dist = qi + WINDOW - kj
    valid = (dist >= 0) & (dist < WINDOW)
    slopes = np.exp2(-8.0 * np.arange(1, ATT_HEADS + 1, dtype=np.float32) / ATT_HEADS)
    bias = np.where(valid[None], -slopes[:, None, None] * dist[None].astype(np.float32), NEG_INF)
    bias = bias.reshape(ATT_KV_HEADS, ATT_GROUP * WINDOW, 2 * WINDOW)
    return jnp.asarray(np.swapaxes(bias, 1, 2), dtype=F32)


def _layer_call(layer, x, mod, p):
    bsz, seq, d = x.shape
    tm = TILE
    assert seq % tm == 0 and d == D_MODEL and bsz <= SUBLANES

    def layer_spec(a):
        nd = a.ndim - 1
        return pl.BlockSpec((None,) + a.shape[1:], lambda b, t: (layer,) + (0,) * nd,
                            pipeline_mode=pl.Buffered(1))

    def const_spec(a):
        nd = a.ndim
        return pl.BlockSpec(a.shape, lambda b, t: (0,) * nd, pipeline_mode=pl.Buffered(1))

    tile_spec = pl.BlockSpec((None, tm, d), lambda b, t: (b, t, 0))
    per_layer = [mod, p["vecs"], p["w_main"], p["w_tail"], p["w_ab"], p["sinks"]]
    after_bias = [p["dw_w"], p["pw2_w"], p["sconv_w"], p["w_proj_a"], p["w_proj_b"], p["w_proj_c"],
                  p["w_out"]]
    return pl.pallas_call(
        _layer_body,
        grid=(bsz, seq // tm),
        in_specs=([tile_spec] + [layer_spec(a) for a in per_layer] + [const_spec(p["bias"])]
                  + [layer_spec(a) for a in after_bias]),
        out_specs=tile_spec,
        out_shape=jax.ShapeDtypeStruct(x.shape, x.dtype),
        scratch_shapes=[
            pltpu.VMEM((ATT_KV_HEADS, WINDOW, GROUP_WIDTH), F32),
            pltpu.VMEM((ATT_KV_WIDTH, WINDOW), BF16),
            pltpu.VMEM((CONV_PAD + tm, CONV_WIDTH), F32),
            pltpu.VMEM((SCONV_PAD + tm, 3 * DN_WIDTH), F32),
            pltpu.VMEM((DN_HEADS, DN_HEAD_DIM, DN_HEAD_DIM), F32),
            pltpu.VMEM((tm, ATT_WIDTH), F32),
            pltpu.VMEM((SUBLANES - 1, tm + CONV_PAD - SUBLANES, CONV_WIDTH), F32),
            pltpu.VMEM((tm, CONV_WIDTH), F32),
            pltpu.VMEM((tm, 3 * DN_WIDTH), F32),
            pltpu.VMEM((tm, DN_WIDTH), F32),
            pltpu.VMEM((tm, 3 * D_MODEL), F32),
        ],
        compiler_params=pltpu.CompilerParams(
            dimension_semantics=("arbitrary", "arbitrary"),
            vmem_limit_bytes=VMEM_LIMIT),
        name="hybrid_layer",
    )(x, *per_layer, p["bias"], *after_bias)


def _prep_params(norm_g, w_in, q_norm_g, k_norm_g, sinks, dw_w, dw_b, ln_g, ln_b, pw2_w, pw2_b,
                 sconv_w, a_log, dt_bias, dn_norm_g, w_proj_a, w_proj_b, w_proj_c, w_out):
    depth, d, _ = w_in.shape

    def row(v):
        return jnp.pad(v, ((0, 0), (0, D_MODEL - v.shape[1])))[:, None, :]

    vec_rows = [None] * N_VECS
    vec_rows[V_NORM_G] = row(norm_g)
    vec_rows[V_Q_GAIN] = row(jnp.tile(q_norm_g, (1, ATT_HEADS)))
    vec_rows[V_K_GAIN] = row(jnp.tile(k_norm_g, (1, ATT_KV_HEADS)))
    vec_rows[V_DW_B] = row(dw_b)
    vec_rows[V_LN_G] = row(ln_g)
    vec_rows[V_LN_B] = row(ln_b)
    vec_rows[V_PW2_B] = row(pw2_b)
    vec_rows[V_A_LOG] = row(a_log)
    vec_rows[V_DT_BIAS] = row(dt_bias)
    vec_rows[V_DN_GAIN] = row(dn_norm_g)
    ab_hi = MAIN_COLS + AB_COLS
    w_in_bf = w_in.astype(BF16)
    return dict(
        vecs=jnp.stack(vec_rows, axis=1),
        w_main=w_in_bf[:, :, :MAIN_COLS],
        w_tail=w_in_bf[:, :, ab_hi:],
        w_ab=jnp.pad(w_in_bf[:, :, MAIN_COLS:ab_hi], ((0, 0), (0, 0), (0, LANES - AB_COLS))),
        sinks=jnp.repeat(sinks, WINDOW, axis=1).reshape(depth, ATT_KV_HEADS, 1, ATT_GROUP * WINDOW),
        bias=_alibi_bias(),
        dw_w=jnp.broadcast_to(dw_w[:, :, None, :], (depth, CONV_K, SUBLANES, CONV_WIDTH)),
        pw2_w=pw2_w.astype(BF16),
        sconv_w=jnp.broadcast_to(sconv_w[:, :, None, :], (depth, DN_CONV_K, SUBLANES, 3 * DN_WIDTH)),
        w_proj_a=w_proj_a.astype(BF16),
        w_proj_b=w_proj_b.astype(BF16),
        w_proj_c=w_proj_c.astype(BF16),
        w_out=w_out.astype(BF16),
    )


@jax.jit
def _forward(x, c, w_ada, b_ada, norm_g, w_in, q_norm_g, k_norm_g, sinks, dw_w, dw_b, ln_g, ln_b,
             pw2_w, pw2_b, sconv_w, a_log, dt_bias, dn_norm_g, w_proj_a, w_proj_b, w_proj_c, w_out):
    bsz, _, d = x.shape
    depth = w_ada.shape[0]
    c_pad = jnp.zeros((SUBLANES, d), F32).at[:bsz].set(c)
    mod = _ada_call(c_pad, w_ada, b_ada)
    p = _prep_params(norm_g, w_in, q_norm_g, k_norm_g, sinks, dw_w, dw_b, ln_g, ln_b, pw2_w, pw2_b,
                     sconv_w, a_log, dt_bias, dn_norm_g, w_proj_a, w_proj_b, w_proj_c, w_out)
    for layer in range(depth):
        x = _layer_call(layer, x, mod, p)
    return x


def kernel(x, c, w_ada, b_ada, norm_g, w_in, q_norm_g, k_norm_g, sinks, dw_w, dw_b, ln_g, ln_b, pw2_w, pw2_b, sconv_w, a_log, dt_bias, dn_norm_g, w_proj_a, w_proj_b, w_proj_c, w_out):
    return _forward(x, c, w_ada, b_ada, norm_g, w_in, q_norm_g, k_norm_g, sinks, dw_w, dw_b, ln_g, ln_b,
                    pw2_w, pw2_b, sconv_w, a_log, dt_bias, dn_norm_g, w_proj_a, w_proj_b, w_proj_c, w_out)
```

```python
import numpy as np
import jax
import jax.numpy as jnp
from jax import lax
from jax.experimental import pallas as pl
from jax.experimental.pallas import tpu as pltpu

F32 = jnp.float32
BF16 = jnp.bfloat16

D_MODEL = 1024
ATT_HEADS = 8
ATT_KV_HEADS = 2
ATT_HEAD_DIM = 64
ATT_GROUP = ATT_HEADS // ATT_KV_HEADS
ATT_WIDTH = ATT_HEADS * ATT_HEAD_DIM
ATT_KV_WIDTH = ATT_KV_HEADS * ATT_HEAD_DIM
GROUP_WIDTH = ATT_GROUP * ATT_HEAD_DIM
WINDOW = 128
CONV_WIDTH = 512
CONV_K = 31
DN_HEADS = 4
DN_HEAD_DIM = 128
DN_WIDTH = DN_HEADS * DN_HEAD_DIM
DN_CONV_K = 4
DN_CHUNK = 64
EPS = 1e-6
NEG_INF = -1e30

LANES = 128
SUBLANES = 8

TILE = 256
CONV_PAD = 32
SCONV_PAD = 8
CONV_ROWS = 32
GATE_COLS = 512
VMEM_LIMIT = 56 * 1024 * 1024

OFF_A = 0
OFF_B = OFF_A + 2 * ATT_WIDTH + 2 * ATT_KV_WIDTH
OFF_C = OFF_B + 3 * CONV_WIDTH
MAIN_COLS = OFF_C + 3 * DN_WIDTH
AB_COLS = 2 * DN_HEADS
TAIL_ZC = 0
TAIL_MG = TAIL_ZC + DN_WIDTH
TAIL_COLS = TAIL_MG + 3 * D_MODEL

(V_NORM_G, V_Q_GAIN, V_K_GAIN, V_DW_B, V_LN_G, V_LN_B, V_PW2_B, V_A_LOG, V_DT_BIAS, V_DN_GAIN,
 N_VECS) = range(11)


def _dot(a, b):
    return jnp.dot(a, b, preferred_element_type=F32)


def _dot_nt(a, b):
    return lax.dot_general(a, b, (((1,), (1,)), ((), ())), preferred_element_type=F32)


def _silu(x):
    return x * jax.nn.sigmoid(x)


def _softplus(x):
    return jnp.maximum(x, 0.0) + jnp.log1p(jnp.exp(-jnp.abs(x)))


def _ada_body(c_ref, w_ref, b_ref, o_ref):
    sc = _silu(c_ref[...]).astype(BF16)
    o_ref[...] = _dot(sc, w_ref[...].astype(BF16)) + b_ref[...]


def _ada_call(c_pad, w_ada, b_ada):
    depth, d, d3 = w_ada.shape
    nblk = d3 // d
    return pl.pallas_call(
        _ada_body,
        grid=(depth, nblk),
        in_specs=[
            pl.BlockSpec((SUBLANES, d), lambda l, j: (0, 0)),
            pl.BlockSpec((None, d, d), lambda l, j: (l, 0, j)),
            pl.BlockSpec((None, 1, d), lambda l, j: (l, 0, j)),
        ],
        out_specs=pl.BlockSpec((None, SUBLANES, d), lambda l, j: (l, 0, j)),
        out_shape=jax.ShapeDtypeStruct((depth, SUBLANES, d3), F32),
        name="adaln_mod",
    )(c_pad, w_ada, b_ada.reshape(depth, 1, d3))


def _layer_body(x_ref, mod_ref, vec_ref, wmain_ref, wtail_ref, wab_ref, sink_ref, bias_ref,
                dww_ref, pw2w_ref, scw_ref, wpa_ref, wpb_ref, wpc_ref, wout_ref,
                o_ref,
                kprev_ref, vprev_ref, convbuf_ref, sconvbuf_ref, state_ref,
                ya_ref, phase_ref, cv_ref, qkv_ref, yc_ref, gates_ref):
    tm = x_ref.shape[0]
    t = pl.program_id(1)

    @pl.when(t == 0)
    def _reset():
        kprev_ref[...] = jnp.zeros_like(kprev_ref)
        vprev_ref[...] = jnp.zeros_like(vprev_ref)
        convbuf_ref[0:CONV_PAD, :] = jnp.zeros((CONV_PAD, CONV_WIDTH), F32)
        sconvbuf_ref[0:SCONV_PAD, :] = jnp.zeros((SCONV_PAD, 3 * DN_WIDTH), F32)
        state_ref[...] = jnp.zeros_like(state_ref)

    def vec(row, width):
        return vec_ref[row, :, 0:width]

    mod = mod_ref[pl.ds(pl.program_id(0), 1), :]
    shift = mod[:, 0:D_MODEL]
    scale = mod[:, D_MODEL:2 * D_MODEL]
    gate = mod[:, 2 * D_MODEL:3 * D_MODEL]
    x = x_ref[...]
    ms = jnp.mean(x * x, axis=-1, keepdims=True)
    h = x * lax.rsqrt(ms + EPS) * vec(V_NORM_G, D_MODEL)
    h = h * (1.0 + scale) + shift
    hb = h.astype(BF16)

    def proj(w_ref, lo, hi):
        return _dot(hb, w_ref[:, lo:hi])

    pa = proj(wmain_ref, OFF_A, OFF_B)
    qa = pa[:, 0:ATT_WIDTH]
    ka = pa[:, ATT_WIDTH:ATT_WIDTH + ATT_KV_WIDTH]
    va = pa[:, ATT_WIDTH + ATT_KV_WIDTH:ATT_WIDTH + 2 * ATT_KV_WIDTH]
    za = pa[:, ATT_WIDTH + 2 * ATT_KV_WIDTH:]

    def head_mean_square(v, width):
        r = lax.broadcasted_iota(jnp.int32, (width, width), 0) // ATT_HEAD_DIM
        c = lax.broadcasted_iota(jnp.int32, (width, width), 1) // ATT_HEAD_DIM
        avg = jnp.where(r == c, 1.0 / ATT_HEAD_DIM, 0.0).astype(BF16)
        sq = v * v
        hi = sq.astype(BF16)
        lo = (sq - hi.astype(F32)).astype(BF16)
        return _dot(hi, avg) + _dot(lo, avg)

    qn = (qa * lax.rsqrt(head_mean_square(qa, ATT_WIDTH) + EPS) * vec(V_Q_GAIN, ATT_WIDTH)
          * (ATT_HEAD_DIM ** -0.5))
    kn = ka * lax.rsqrt(head_mean_square(ka, ATT_KV_WIDTH) + EPS) * vec(V_K_GAIN, ATT_KV_WIDTH)

    low_half = lax.broadcasted_iota(jnp.int32, (tm, ATT_KV_WIDTH), 1) < ATT_HEAD_DIM
    k_swapped = pltpu.roll(kn, ATT_HEAD_DIM, axis=1)
    krep = [jnp.concatenate([kh, kh], axis=1).astype(BF16)
            for kh in (jnp.where(low_half, kn, k_swapped), jnp.where(low_half, k_swapped, kn))]
    v_t = va.T.astype(BF16)

    q_head = lax.broadcasted_iota(jnp.int32, (WINDOW, GROUP_WIDTH), 1) // ATT_HEAD_DIM
    key_row = lax.broadcasted_iota(jnp.int32, (2 * WINDOW, ATT_GROUP * WINDOW), 0)
    n_masked_rows = jnp.where(t == 0, WINDOW, 0)

    blocks = [(j, g) for j in range(tm // WINDOW) for g in range(ATT_KV_HEADS)]
    scores, v_cats = {}, {}
    for j, g in blocks:
        rows = slice(j * WINDOW, (j + 1) * WINDOW)
        if j == 0:
            k_prev, v_prev = kprev_ref[g].astype(BF16), vprev_ref[...]
        else:
            prev_rows = slice((j - 1) * WINDOW, j * WINDOW)
            k_prev, v_prev = krep[g][prev_rows], v_t[:, prev_rows]
        kcat = jnp.concatenate([k_prev, krep[g][rows]], axis=0)
        v_cats[j] = jnp.concatenate([v_prev, v_t[:, rows]], axis=1)
        qg = qn[rows, g * GROUP_WIDTH:(g + 1) * GROUP_WIDTH]
        qstack = jnp.concatenate(
            [jnp.where(q_head == i, qg, 0.0) for i in range(ATT_GROUP)], axis=0).astype(BF16)
        s = _dot_nt(kcat, qstack) + bias_ref[g]
        if j == 0:
            s = jnp.where(key_row < n_masked_rows, NEG_INF, s)
        scores[j, g] = s
    probs = {}
    for j, g in blocks:
        s = scores[j, g]
        sink = sink_ref[g]
        m = jnp.maximum(jnp.max(s, axis=0, keepdims=True), sink)
        p = jnp.exp(s - m)
        denom = jnp.sum(p, axis=0, keepdims=True) + jnp.exp(sink - m)
        probs[j, g] = (p * (1.0 / denom)).astype(BF16)
    for j, g in blocks:
        o_t = _dot(v_cats[j][g * ATT_HEAD_DIM:(g + 1) * ATT_HEAD_DIM], probs[j, g])
        stacked = jnp.concatenate([o_t[:, i * WINDOW:(i + 1) * WINDOW] for i in range(ATT_GROUP)], axis=0)
        ya_ref[j * WINDOW:(j + 1) * WINDOW, g * GROUP_WIDTH:(g + 1) * GROUP_WIDTH] = stacked.T

    for g in range(ATT_KV_HEADS):
        kprev_ref[g] = krep[g][tm - WINDOW:tm].astype(F32)
    vprev_ref[...] = v_t[:, tm - WINDOW:tm]

    ya = ya_ref[...] * _silu(za)

    pb = proj(wmain_ref, OFF_B, OFF_C)
    ub = pb[:, 0:CONV_WIDTH] * jax.nn.sigmoid(pb[:, CONV_WIDTH:2 * CONV_WIDTH])
    zb = pb[:, 2 * CONV_WIDTH:]
    convbuf_ref[CONV_PAD:CONV_PAD + tm, :] = ub
    base = CONV_PAD - (CONV_K - 1)
    phase_rows = phase_ref.shape[1]
    for ph in range(1, SUBLANES):
        phase_ref[ph - 1] = convbuf_ref[ph:ph + phase_rows, :]
    sconvbuf_ref[SCONV_PAD:SCONV_PAD + tm, :] = proj(wmain_ref, OFF_C, MAIN_COLS)
    for piece in range(3 * D_MODEL // GATE_COLS):
        cols = slice(piece * GATE_COLS, (piece + 1) * GATE_COLS)
        gates_ref[:, cols] = jax.nn.sigmoid(
            proj(wtail_ref, TAIL_MG + piece * GATE_COLS, TAIL_MG + (piece + 1) * GATE_COLS))
    dw_b = vec(V_DW_B, CONV_WIDTH)
    for r in range(tm // CONV_ROWS):
        acc = jnp.broadcast_to(dw_b, (CONV_ROWS, CONV_WIDTH))
        for k in range(CONV_K):
            ph = (base + k) % SUBLANES
            start = (base + k) - ph + r * CONV_ROWS
            if ph == 0:
                win = convbuf_ref[start:start + CONV_ROWS, :]
            else:
                win = phase_ref[ph - 1, start:start + CONV_ROWS, :]
            acc = acc + jnp.concatenate([dww_ref[k]] * (CONV_ROWS // SUBLANES), axis=0) * win
        cv_ref[r * CONV_ROWS:(r + 1) * CONV_ROWS, :] = acc
    convbuf_ref[0:CONV_PAD, :] = convbuf_ref[tm:tm + CONV_PAD, :]
    cv = cv_ref[...]
    mu = jnp.mean(cv, axis=-1, keepdims=True)
    xc = cv - mu
    var = jnp.mean(xc * xc, axis=-1, keepdims=True)
    ln = xc * lax.rsqrt(var + EPS) * vec(V_LN_G, CONV_WIDTH) + vec(V_LN_B, CONV_WIDTH)
    yb = (_dot(_silu(ln).astype(BF16), pw2w_ref[...]) + vec(V_PW2_B, CONV_WIDTH)) * _silu(zb)
    merged = _dot(ya.astype(BF16), wpa_ref[...]) * gates_ref[:, 0:D_MODEL]
    merged = merged + _dot(yb.astype(BF16), wpb_ref[...]) * gates_ref[:, D_MODEL:2 * D_MODEL]

    sbase = SCONV_PAD - (DN_CONV_K - 1)
    for r in range(tm // CONV_ROWS):
        for cblk in range(3):
            cols = slice(cblk * DN_WIDTH, (cblk + 1) * DN_WIDTH)
            acc = None
            for k in range(DN_CONV_K):
                start = sbase + k + r * CONV_ROWS
                tap = jnp.concatenate([scw_ref[k, :, cols]] * (CONV_ROWS // SUBLANES), axis=0)
                term = tap * sconvbuf_ref[start:start + CONV_ROWS, cols]
                acc = term if acc is None else acc + term
            qkv_ref[r * CONV_ROWS:(r + 1) * CONV_ROWS, cols] = _silu(acc)
    sconvbuf_ref[0:SCONV_PAD, :] = sconvbuf_ref[tm:tm + SCONV_PAD, :]

    ab = _dot(hb, wab_ref[...])
    g_col = -jnp.exp(vec(V_A_LOG, LANES)) * _softplus(ab + vec(V_DT_BIAS, LANES))
    beta_col = jax.nn.sigmoid(ab)

    n_chunks = tm // DN_CHUNK
    ri = lax.broadcasted_iota(jnp.int32, (tm, tm), 0)
    ci = lax.broadcasted_iota(jnp.int32, (tm, tm), 1)
    same_chunk = (ri // DN_CHUNK) == (ci // DN_CHUNK)
    g_hi = g_col.astype(BF16)
    g_mid = (g_col - g_hi.astype(F32)).astype(BF16)
    g_lo = (g_col - g_hi.astype(F32) - g_mid.astype(F32)).astype(BF16)
    tri = jnp.where(same_chunk & (ri >= ci), 1.0, 0.0).astype(BF16)
    ones = jnp.where(same_chunk, 1.0, 0.0).astype(BF16)
    gc_col = _dot(tri, g_hi) + _dot(tri, g_mid) + _dot(tri, g_lo)
    gl_col = _dot(ones, g_hi) + _dot(ones, g_mid) + _dot(ones, g_lo)
    gc_row = gc_col.T

    heads = range(DN_HEADS)
    chunks = range(n_chunks)
    crows = [slice(c * DN_CHUNK, (c + 1) * DN_CHUNK) for c in chunks]
    cat_w = n_chunks * DN_CHUNK
    cat_row = lax.broadcasted_iota(jnp.int32, (DN_CHUNK, cat_w), 0)
    cat_lane = lax.broadcasted_iota(jnp.int32, (DN_CHUNK, cat_w), 1)
    cat_blk = cat_lane // DN_CHUNK
    cat_col = cat_lane - cat_blk * DN_CHUNK
    lower = cat_row >= cat_col
    strict = cat_row > cat_col
    diag_blk = (lax.broadcasted_iota(jnp.int32, (cat_w, cat_w), 0) // DN_CHUNK
                == lax.broadcasted_iota(jnp.int32, (cat_w, cat_w), 1) // DN_CHUNK)
    key_blk = (lax.broadcasted_iota(jnp.int32, (tm, n_chunks * DN_HEAD_DIM), 0) // DN_CHUNK
               == lax.broadcasted_iota(jnp.int32, (tm, n_chunks * DN_HEAD_DIM), 1) // DN_HEAD_DIM)

    def block_diag(m_cat):
        tiled = jnp.concatenate([m_cat] * n_chunks, axis=0)
        return jnp.where(diag_blk, tiled, jnp.zeros_like(tiled))

    def chunk_cat(v):
        return jnp.concatenate([v[crows[c]] for c in chunks], axis=1)

    qe_l, ke_l, xin_l, gl_l, sc_l, decay_l = [], [], [], [], [], []
    for hd in heads:
        qh = qkv_ref[:, hd * DN_HEAD_DIM:(hd + 1) * DN_HEAD_DIM]
        kh = qkv_ref[:, DN_WIDTH + hd * DN_HEAD_DIM:DN_WIDTH + (hd + 1) * DN_HEAD_DIM]
        vh = qkv_ref[:, 2 * DN_WIDTH + hd * DN_HEAD_DIM:2 * DN_WIDTH + (hd + 1) * DN_HEAD_DIM]
        qh = qh * lax.rsqrt(jnp.sum(qh * qh, axis=-1, keepdims=True) + EPS) * (DN_HEAD_DIM ** -0.5)
        kh = kh * lax.rsqrt(jnp.sum(kh * kh, axis=-1, keepdims=True) + EPS)
        beta = beta_col[:, DN_HEADS + hd:DN_HEADS + hd + 1]
        gc = gc_col[:, hd:hd + 1]
        gl = gl_col[:, hd:hd + 1]
        kb = kh * beta
        eg = jnp.exp(gc)
        qe_l.append(qh * eg)
        ke_l.append(kh * jnp.exp(gl - gc))
        xin_l.append(jnp.concatenate([vh * beta, kb * eg], axis=1))
        gl_l.append(gl)
        kq = jnp.concatenate([chunk_cat(kb), chunk_cat(qh)], axis=0).astype(BF16)
        k_tiled = jnp.concatenate([kh.astype(BF16)] * n_chunks, axis=1)
        k_bd = jnp.where(key_blk, k_tiled, jnp.zeros_like(k_tiled))
        sc_l.append(_dot_nt(kq, k_bd))
        gc_i = gc[crows[n_chunks - 1]]
        for c in reversed(range(n_chunks - 1)):
            gc_i = jnp.where(cat_blk == c, gc[crows[c]], gc_i)
        decay_l.append(jnp.exp(jnp.where(lower, gc_i - gc_row[hd:hd + 1, :], NEG_INF)))

    pw_l = [jnp.where(strict, sc_l[hd][0:DN_CHUNK] * decay_l[hd], 0.0) for hd in heads]
    intra_l = [(sc_l[hd][DN_CHUNK:] * decay_l[hd]).astype(BF16) for hd in heads]
    q_l = [-pw_l[hd] for hd in heads]
    n_steps = int(np.log2(DN_CHUNK))
    for step in range(n_steps):
        bd_l = [block_diag(pw_l[hd].astype(BF16)) for hd in heads]
        if step == 0:
            pw_l = [_dot(pw_l[hd].astype(BF16), bd_l[hd]) for hd in heads]
            continue
        last = step == n_steps - 1
        lhs_l = [q_l[hd].astype(BF16) if last else
                 jnp.concatenate([pw_l[hd], q_l[hd]], axis=0).astype(BF16) for hd in heads]
        r_l = [_dot(lhs_l[hd], bd_l[hd]) for hd in heads]
        q_l = [q_l[hd] + pw_l[hd] + r_l[hd][-DN_CHUNK:] for hd in heads]
        if not last:
            pw_l = [r_l[hd][0:DN_CHUNK] for hd in heads]
    uw_l = [xin_l[hd] + _dot(block_diag(q_l[hd].astype(BF16)), xin_l[hd].astype(BF16)) for hd in heads]

    state = [state_ref[hd] for hd in heads]
    o_inter = [[] for _ in heads]
    v_news = [[] for _ in heads]
    for c in chunks:
        res = []
        for hd in heads:
            wq = jnp.concatenate([uw_l[hd][crows[c], DN_HEAD_DIM:], qe_l[hd][crows[c]]], axis=0).astype(BF16)
            res.append(_dot(wq, state[hd].astype(BF16)))
        for hd in heads:
            v_new = uw_l[hd][crows[c], 0:DN_HEAD_DIM] - res[hd][0:DN_CHUNK]
            v_news[hd].append(v_new)
            o_inter[hd].append(res[hd][DN_CHUNK:])
            g_last = gl_l[hd][c * DN_CHUNK:c * DN_CHUNK + 1, :]
            state[hd] = state[hd] * jnp.exp(g_last) + _dot(ke_l[hd][crows[c]].T.astype(BF16),
                                                           v_new.astype(BF16))
    dn_gain = vec(V_DN_GAIN, DN_HEAD_DIM)
    for hd in heads:
        state_ref[hd] = state[hd]
        v_new_all = jnp.concatenate(v_news[hd], axis=0).astype(BF16)
        od = jnp.concatenate(o_inter[hd], axis=0) + _dot(block_diag(intra_l[hd]), v_new_all)
        od = od * lax.rsqrt(jnp.mean(od * od, axis=-1, keepdims=True) + EPS) * dn_gain
        yc_ref[:, hd * DN_HEAD_DIM:(hd + 1) * DN_HEAD_DIM] = od

    yc = yc_ref[...] * _silu(proj(wtail_ref, TAIL_ZC, TAIL_MG))
    merged = merged + _dot(yc.astype(BF16), wpc_ref[...]) * gates_ref[:, 2 * D_MODEL:3 * D_MODEL]

    o_ref[...] = x + gate * _dot(merged.astype(BF16), wout_ref[...])


def _alibi_bias():
    qi = np.arange(WINDOW)[:, None]
    kj = np.arange(2 * WINDOW)[None, :]
    dist = qi + WINDOW - kj
    valid = (dist >= 0) & (dist < WINDOW)
    slopes = np.exp2(-8.0 * np.arange(1, ATT_HEADS + 1, dtype=np.float32) / ATT_HEADS)
    bias = np.where(valid[None], -slopes[:, None, None] * dist[None].astype(np.float32), NEG_INF)
    bias = bias.reshape(ATT_KV_HEADS, ATT_GROUP * WINDOW, 2 * WINDOW)
    return jnp.asarray(np.swapaxes(bias, 1, 2), dtype=F32)


def _layer_call(layer, x, mod, p):
    bsz, seq, d = x.shape
    tm = TILE
    assert seq % tm == 0 and d == D_MODEL and bsz <= SUBLANES

    def layer_spec(a):
        nd = a.ndim - 1
        return pl.BlockSpec((None,) + a.shape[1:], lambda b, t: (layer,) + (0,) * nd,
                            pipeline_mode=pl.Buffered(1))

    def const_spec(a):
        nd = a.ndim
        return pl.BlockSpec(a.shape, lambda b, t: (0,) * nd, pipeline_mode=pl.Buffered(1))

    tile_spec = pl.BlockSpec((None, tm, d), lambda b, t: (b, t, 0))
    per_layer = [mod, p["vecs"], p["w_main"], p["w_tail"], p["w_ab"], p["sinks"]]
    after_bias = [p["dw_w"], p["pw2_w"], p["sconv_w"], p["w_proj_a"], p["w_proj_b"], p["w_proj_c"],
                  p["w_out"]]
    return pl.pallas_call(
        _layer_body,
        grid=(bsz, seq // tm),
        in_specs=([tile_spec] + [layer_spec(a) for a in per_layer] + [const_spec(p["bias"])]
                  + [layer_spec(a) for a in after_bias]),
        out_specs=tile_spec,
        out_shape=jax.ShapeDtypeStruct(x.shape, x.dtype),
        scratch_shapes=[
            pltpu.VMEM((ATT_KV_HEADS, WINDOW, GROUP_WIDTH), F32),
            pltpu.VMEM((ATT_KV_WIDTH, WINDOW), BF16),
            pltpu.VMEM((CONV_PAD + tm, CONV_WIDTH), F32),
            pltpu.VMEM((SCONV_PAD + tm, 3 * DN_WIDTH), F32),
            pltpu.VMEM((DN_HEADS, DN_HEAD_DIM, DN_HEAD_DIM), F32),
            pltpu.VMEM((tm, ATT_WIDTH), F32),
            pltpu.VMEM((SUBLANES - 1, tm + CONV_PAD - SUBLANES, CONV_WIDTH), F32),
            pltpu.VMEM((tm, CONV_WIDTH), F32),
            pltpu.VMEM((tm, 3 * DN_WIDTH), F32),
            pltpu.VMEM((tm, DN_WIDTH), F32),
            pltpu.VMEM((tm, 3 * D_MODEL), F32),
        ],
        compiler_params=pltpu.CompilerParams(
            dimension_semantics=("arbitrary", "arbitrary"),
            vmem_limit_bytes=VMEM_LIMIT),
        name="hybrid_layer",
    )(x, *per_layer, p["bias"], *after_bias)


def _prep_params(norm_g, w_in, q_norm_g, k_norm_g, sinks, dw_w, dw_b, ln_g, ln_b, pw2_w, pw2_b,
                 sconv_w, a_log, dt_bias, dn_norm_g, w_proj_a, w_proj_b, w_proj_c, w_out):
    depth, d, _ = w_in.shape

    def row(v):
        return jnp.pad(v, ((0, 0), (0, D_MODEL - v.shape[1])))[:, None, :]

    vec_rows = [None] * N_VECS
    vec_rows[V_NORM_G] = row(norm_g)
    vec_rows[V_Q_GAIN] = row(jnp.tile(q_norm_g, (1, ATT_HEADS)))
    vec_rows[V_K_GAIN] = row(jnp.tile(k_norm_g, (1, ATT_KV_HEADS)))
    vec_rows[V_DW_B] = row(dw_b)
    vec_rows[V_LN_G] = row(ln_g)
    vec_rows[V_LN_B] = row(ln_b)
    vec_rows[V_PW2_B] = row(pw2_b)
    vec_rows[V_A_LOG] = row(a_log)
    vec_rows[V_DT_BIAS] = row(dt_bias)
    vec_rows[V_DN_GAIN] = row(dn_norm_g)
    ab_hi = MAIN_COLS + AB_COLS
    w_in_bf = lax.optimization_barrier(w_in.astype(BF16))
    return dict(
        vecs=jnp.stack(vec_rows, axis=1),
        w_main=w_in_bf[:, :, :MAIN_COLS],
        w_tail=w_in_bf[:, :, ab_hi:],
        w_ab=jnp.pad(w_in_bf[:, :, MAIN_COLS:ab_hi], ((0, 0), (0, 0), (0, LANES - AB_COLS))),
        sinks=jnp.repeat(sinks, WINDOW, axis=1).reshape(depth, ATT_KV_HEADS, 1, ATT_GROUP * WINDOW),
        bias=_alibi_bias(),
        dw_w=jnp.broadcast_to(dw_w[:, :, None, :], (depth, CONV_K, SUBLANES, CONV_WIDTH)),
        pw2_w=pw2_w.astype(BF16),
        sconv_w=jnp.broadcast_to(sconv_w[:, :, None, :], (depth, DN_CONV_K, SUBLANES, 3 * DN_WIDTH)),
        w_proj_a=w_proj_a.astype(BF16),
        w_proj_b=w_proj_b.astype(BF16),
        w_proj_c=w_proj_c.astype(BF16),
        w_out=w_out.astype(BF16),
    )


@jax.jit
def _forward(x, c, w_ada, b_ada, norm_g, w_in, q_norm_g, k_norm_g, sinks, dw_w, dw_b, ln_g, ln_b,
             pw2_w, pw2_b, sconv_w, a_log, dt_bias, dn_norm_g, w_proj_a, w_proj_b, w_proj_c, w_out):
    bsz, _, d = x.shape
    depth = w_ada.shape[0]
    c_pad = jnp.zeros((SUBLANES, d), F32).at[:bsz].set(c)
    mod = _ada_call(c_pad, w_ada, b_ada)
    p = _prep_params(norm_g, w_in, q_norm_g, k_norm_g, sinks, dw_w, dw_b, ln_g, ln_b, pw2_w, pw2_b,
                     sconv_w, a_log, dt_bias, dn_norm_g, w_proj_a, w_proj_b, w_proj_c, w_out)
    for layer in range(depth):
        x = _layer_call(layer, x, mod, p)
    return x


def kernel(x, c, w_ada, b_ada, norm_g, w_in, q_norm_g, k_norm_g, sinks, dw_w, dw_b, ln_g, ln_b, pw2_w, pw2_b, sconv_w, a_log, dt_bias, dn_norm_g, w_proj_a, w_proj_b, w_proj_c, w_out):
    return _forward(x, c, w_ada, b_ada, norm_g, w_in, q_norm_g, k_norm_g, sinks, dw_w, dw_b, ln_g, ln_b,
                    pw2_w, pw2_b, sconv_w, a_log, dt_bias, dn_norm_g, w_proj_a, w_proj_b, w_proj_c, w_out)
```

```python
import numpy as np
import jax
import jax.numpy as jnp
from jax import lax
from jax.experimental import pallas as pl
from jax.experimental.pallas import tpu as pltpu

F32 = jnp.float32
BF16 = jnp.bfloat16

D_MODEL = 1024
ATT_HEADS = 8
ATT_KV_HEADS = 2
ATT_HEAD_DIM = 64
ATT_GROUP = ATT_HEADS // ATT_KV_HEADS
ATT_WIDTH = ATT_HEADS * ATT_HEAD_DIM
ATT_KV_WIDTH = ATT_KV_HEADS * ATT_HEAD_DIM
GROUP_WIDTH = ATT_GROUP * ATT_HEAD_DIM
WINDOW = 128
CONV_WIDTH = 512
CONV_K = 31
DN_HEADS = 4
DN_HEAD_DIM = 128
DN_WIDTH = DN_HEADS * DN_HEAD_DIM
DN_CONV_K = 4
DN_CHUNK = 64
EPS = 1e-6
NEG_INF = -1e30

LANES = 128
SUBLANES = 8

TILE = 256
CONV_PAD = 32
SCONV_PAD = 8
CONV_ROWS = 32
GATE_COLS = 512
TAIL_SHIFT_COLS = 512
VMEM_LIMIT = 56 * 1024 * 1024

OFF_A = 0
OFF_B = OFF_A + 2 * ATT_WIDTH + 2 * ATT_KV_WIDTH
OFF_C = OFF_B + 3 * CONV_WIDTH
MAIN_COLS = OFF_C + 3 * DN_WIDTH
AB_COLS = 2 * DN_HEADS
TAIL_ZC = 0
TAIL_MG = TAIL_ZC + DN_WIDTH
TAIL_COLS = TAIL_MG + 3 * D_MODEL

(V_NORM_G, V_Q_GAIN, V_K_GAIN, V_DW_B, V_LN_G, V_LN_B, V_PW2_B, V_A_LOG, V_DT_BIAS, V_DN_GAIN,
 N_VECS) = range(11)


def _dot(a, b):
    return jnp.dot(a, b, preferred_element_type=F32)


def _dot_nt(a, b):
    return lax.dot_general(a, b, (((1,), (1,)), ((), ())), preferred_element_type=F32)


def _silu(x):
    return x * jax.nn.sigmoid(x)


def _softplus(x):
    return jnp.maximum(x, 0.0) + jnp.log1p(jnp.exp(-jnp.abs(x)))


def _ada_body(c_ref, w_ref, b_ref, o_ref):
    sc = _silu(c_ref[...]).astype(BF16)
    o_ref[...] = _dot(sc, w_ref[...].astype(BF16)) + b_ref[...]


def _ada_call(c_pad, w_ada, b_ada):
    depth, d, d3 = w_ada.shape
    nblk = d3 // d
    return pl.pallas_call(
        _ada_body,
        grid=(depth, nblk),
        in_specs=[
            pl.BlockSpec((SUBLANES, d), lambda l, j: (0, 0)),
            pl.BlockSpec((None, d, d), lambda l, j: (l, 0, j)),
            pl.BlockSpec((None, 1, d), lambda l, j: (l, 0, j)),
        ],
        out_specs=pl.BlockSpec((None, SUBLANES, d), lambda l, j: (l, 0, j)),
        out_shape=jax.ShapeDtypeStruct((depth, SUBLANES, d3), F32),
        name="adaln_mod",
    )(c_pad, w_ada, b_ada.reshape(depth, 1, d3))


def _layer_body(x_ref, mod_ref, vec_ref, wmain_ref, wrest_ref, sink_ref, bias_ref,
                dww_ref, pw2w_ref, scw_ref, wpa_ref, wpb_ref, wpc_ref, wout_ref,
                o_ref,
                kprev_ref, vprev_ref, convbuf_ref, sconvbuf_ref, state_ref,
                ya_ref, phase_ref, cv_ref, qkv_ref, yc_ref, gates_ref, wtail_ref):
    tm = x_ref.shape[0]
    t = pl.program_id(1)

    @pl.when((t == 0) & (pl.program_id(0) == 0))
    def _align_tail_weights():
        for lo in range(0, TAIL_COLS, TAIL_SHIFT_COLS):
            window = wrest_ref[:, lo:lo + TAIL_SHIFT_COLS + LANES].astype(F32)
            moved = pltpu.roll(window, TAIL_SHIFT_COLS + LANES - AB_COLS, axis=1)
            wtail_ref[:, lo:lo + TAIL_SHIFT_COLS] = moved[:, 0:TAIL_SHIFT_COLS].astype(BF16)

    @pl.when(t == 0)
    def _reset():
        kprev_ref[...] = jnp.zeros_like(kprev_ref)
        vprev_ref[...] = jnp.zeros_like(vprev_ref)
        convbuf_ref[0:CONV_PAD, :] = jnp.zeros((CONV_PAD, CONV_WIDTH), F32)
        sconvbuf_ref[0:SCONV_PAD, :] = jnp.zeros((SCONV_PAD, 3 * DN_WIDTH), F32)
        state_ref[...] = jnp.zeros_like(state_ref)

    def vec(row, width):
        return vec_ref[row, :, 0:width]

    mod = mod_ref[pl.ds(pl.program_id(0), 1), :]
    shift = mod[:, 0:D_MODEL]
    scale = mod[:, D_MODEL:2 * D_MODEL]
    gate = mod[:, 2 * D_MODEL:3 * D_MODEL]
    x = x_ref[...]
    ms = jnp.mean(x * x, axis=-1, keepdims=True)
    h = x * lax.rsqrt(ms + EPS) * vec(V_NORM_G, D_MODEL)
    h = h * (1.0 + scale) + shift
    hb = h.astype(BF16)

    def proj(w_ref, lo, hi):
        return _dot(hb, w_ref[:, lo:hi])

    pa = proj(wmain_ref, OFF_A, OFF_B)
    qa = pa[:, 0:ATT_WIDTH]
    ka = pa[:, ATT_WIDTH:ATT_WIDTH + ATT_KV_WIDTH]
    va = pa[:, ATT_WIDTH + ATT_KV_WIDTH:ATT_WIDTH + 2 * ATT_KV_WIDTH]
    za = pa[:, ATT_WIDTH + 2 * ATT_KV_WIDTH:]

    def head_mean_square(v, width):
        r = lax.broadcasted_iota(jnp.int32, (width, width), 0) // ATT_HEAD_DIM
        c = lax.broadcasted_iota(jnp.int32, (width, width), 1) // ATT_HEAD_DIM
        avg = jnp.where(r == c, 1.0 / ATT_HEAD_DIM, 0.0).astype(BF16)
        sq = v * v
        hi = sq.astype(BF16)
        lo = (sq - hi.astype(F32)).astype(BF16)
        return _dot(hi, avg) + _dot(lo, avg)

    qn = (qa * lax.rsqrt(head_mean_square(qa, ATT_WIDTH) + EPS) * vec(V_Q_GAIN, ATT_WIDTH)
          * (ATT_HEAD_DIM ** -0.5))
    kn = ka * lax.rsqrt(head_mean_square(ka, ATT_KV_WIDTH) + EPS) * vec(V_K_GAIN, ATT_KV_WIDTH)

    low_half = lax.broadcasted_iota(jnp.int32, (tm, ATT_KV_WIDTH), 1) < ATT_HEAD_DIM
    k_swapped = pltpu.roll(kn, ATT_HEAD_DIM, axis=1)
    krep = [jnp.concatenate([kh, kh], axis=1).astype(BF16)
            for kh in (jnp.where(low_half, kn, k_swapped), jnp.where(low_half, k_swapped, kn))]
    v_t = va.T.astype(BF16)

    q_head = lax.broadcasted_iota(jnp.int32, (WINDOW, GROUP_WIDTH), 1) // ATT_HEAD_DIM
    key_row = lax.broadcasted_iota(jnp.int32, (2 * WINDOW, ATT_GROUP * WINDOW), 0)
    n_masked_rows = jnp.where(t == 0, WINDOW, 0)

    blocks = [(j, g) for j in range(tm // WINDOW) for g in range(ATT_KV_HEADS)]
    scores, v_cats = {}, {}
    for j, g in blocks:
        rows = slice(j * WINDOW, (j + 1) * WINDOW)
        if j == 0:
            k_prev, v_prev = kprev_ref[g].astype(BF16), vprev_ref[...]
        else:
            prev_rows = slice((j - 1) * WINDOW, j * WINDOW)
            k_prev, v_prev = krep[g][prev_rows], v_t[:, prev_rows]
        kcat = jnp.concatenate([k_prev, krep[g][rows]], axis=0)
        v_cats[j] = jnp.concatenate([v_prev, v_t[:, rows]], axis=1)
        qg = qn[rows, g * GROUP_WIDTH:(g + 1) * GROUP_WIDTH]
        qstack = jnp.concatenate(
            [jnp.where(q_head == i, qg, 0.0) for i in range(ATT_GROUP)], axis=0).astype(BF16)
        s = _dot_nt(kcat, qstack) + bias_ref[g]
        if j == 0:
            s = jnp.where(key_row < n_masked_rows, NEG_INF, s)
        scores[j, g] = s
    probs = {}
    for j, g in blocks:
        s = scores[j, g]
        sink = sink_ref[g]
        m = jnp.maximum(jnp.max(s, axis=0, keepdims=True), sink)
        p = jnp.exp(s - m)
        denom = jnp.sum(p, axis=0, keepdims=True) + jnp.exp(sink - m)
        probs[j, g] = (p * (1.0 / denom)).astype(BF16)
    for j, g in blocks:
        o_t = _dot(v_cats[j][g * ATT_HEAD_DIM:(g + 1) * ATT_HEAD_DIM], probs[j, g])
        stacked = jnp.concatenate([o_t[:, i * WINDOW:(i + 1) * WINDOW] for i in range(ATT_GROUP)], axis=0)
        ya_ref[j * WINDOW:(j + 1) * WINDOW, g * GROUP_WIDTH:(g + 1) * GROUP_WIDTH] = stacked.T

    for g in range(ATT_KV_HEADS):
        kprev_ref[g] = krep[g][tm - WINDOW:tm].astype(F32)
    vprev_ref[...] = v_t[:, tm - WINDOW:tm]

    ya = ya_ref[...] * _silu(za)

    pb = proj(wmain_ref, OFF_B, OFF_C)
    ub = pb[:, 0:CONV_WIDTH] * jax.nn.sigmoid(pb[:, CONV_WIDTH:2 * CONV_WIDTH])
    zb = pb[:, 2 * CONV_WIDTH:]
    convbuf_ref[CONV_PAD:CONV_PAD + tm, :] = ub
    base = CONV_PAD - (CONV_K - 1)
    phase_rows = phase_ref.shape[1]
    for ph in range(1, SUBLANES):
        phase_ref[ph - 1] = convbuf_ref[ph:ph + phase_rows, :]
    sconvbuf_ref[SCONV_PAD:SCONV_PAD + tm, :] = proj(wmain_ref, OFF_C, MAIN_COLS)
    for piece in range(3 * D_MODEL // GATE_COLS):
        cols = slice(piece * GATE_COLS, (piece + 1) * GATE_COLS)
        gates_ref[:, cols] = jax.nn.sigmoid(
            proj(wtail_ref, TAIL_MG + piece * GATE_COLS, TAIL_MG + (piece + 1) * GATE_COLS))
    dw_b = vec(V_DW_B, CONV_WIDTH)
    for r in range(tm // CONV_ROWS):
        acc = jnp.broadcast_to(dw_b, (CONV_ROWS, CONV_WIDTH))
        for k in range(CONV_K):
            ph = (base + k) % SUBLANES
            start = (base + k) - ph + r * CONV_ROWS
            if ph == 0:
                win = convbuf_ref[start:start + CONV_ROWS, :]
            else:
                win = phase_ref[ph - 1, start:start + CONV_ROWS, :]
            acc = acc + jnp.concatenate([dww_ref[k]] * (CONV_ROWS // SUBLANES), axis=0) * win
        cv_ref[r * CONV_ROWS:(r + 1) * CONV_ROWS, :] = acc
    convbuf_ref[0:CONV_PAD, :] = convbuf_ref[tm:tm + CONV_PAD, :]
    cv = cv_ref[...]
    mu = jnp.mean(cv, axis=-1, keepdims=True)
    xc = cv - mu
    var = jnp.mean(xc * xc, axis=-1, keepdims=True)
    ln = xc * lax.rsqrt(var + EPS) * vec(V_LN_G, CONV_WIDTH) + vec(V_LN_B, CONV_WIDTH)
    yb = (_dot(_silu(ln).astype(BF16), pw2w_ref[...]) + vec(V_PW2_B, CONV_WIDTH)) * _silu(zb)
    merged = _dot(ya.astype(BF16), wpa_ref[...]) * gates_ref[:, 0:D_MODEL]
    merged = merged + _dot(yb.astype(BF16), wpb_ref[...]) * gates_ref[:, D_MODEL:2 * D_MODEL]

    sbase = SCONV_PAD - (DN_CONV_K - 1)
    for r in range(tm // CONV_ROWS):
        for cblk in range(3):
            cols = slice(cblk * DN_WIDTH, (cblk + 1) * DN_WIDTH)
            acc = None
            for k in range(DN_CONV_K):
                start = sbase + k + r * CONV_ROWS
                tap = jnp.concatenate([scw_ref[k, :, cols]] * (CONV_ROWS // SUBLANES), axis=0)
                term = tap * sconvbuf_ref[start:start + CONV_ROWS, cols]
                acc = term if acc is None else acc + term
            qkv_ref[r * CONV_ROWS:(r + 1) * CONV_ROWS, cols] = _silu(acc)
    sconvbuf_ref[0:SCONV_PAD, :] = sconvbuf_ref[tm:tm + SCONV_PAD, :]

    ab = _dot(hb, wrest_ref[:, 0:LANES])
    g_col = -jnp.exp(vec(V_A_LOG, LANES)) * _softplus(ab + vec(V_DT_BIAS, LANES))
    beta_col = jax.nn.sigmoid(ab)

    n_chunks = tm // DN_CHUNK
    ri = lax.broadcasted_iota(jnp.int32, (tm, tm), 0)
    ci = lax.broadcasted_iota(jnp.int32, (tm, tm), 1)
    same_chunk = (ri // DN_CHUNK) == (ci // DN_CHUNK)
    g_hi = g_col.astype(BF16)
    g_mid = (g_col - g_hi.astype(F32)).astype(BF16)
    g_lo = (g_col - g_hi.astype(F32) - g_mid.astype(F32)).astype(BF16)
    tri = jnp.where(same_chunk & (ri >= ci), 1.0, 0.0).astype(BF16)
    ones = jnp.where(same_chunk, 1.0, 0.0).astype(BF16)
    gc_col = _dot(tri, g_hi) + _dot(tri, g_mid) + _dot(tri, g_lo)
    gl_col = _dot(ones, g_hi) + _dot(ones, g_mid) + _dot(ones, g_lo)
    gc_row = gc_col.T

    heads = range(DN_HEADS)
    chunks = range(n_chunks)
    crows = [slice(c * DN_CHUNK, (c + 1) * DN_CHUNK) for c in chunks]
    cat_w = n_chunks * DN_CHUNK
    cat_row = lax.broadcasted_iota(jnp.int32, (DN_CHUNK, cat_w), 0)
    cat_lane = lax.broadcasted_iota(jnp.int32, (DN_CHUNK, cat_w), 1)
    cat_blk = cat_lane // DN_CHUNK
    cat_col = cat_lane - cat_blk * DN_CHUNK
    lower = cat_row >= cat_col
    strict = cat_row > cat_col
    diag_blk = (lax.broadcasted_iota(jnp.int32, (cat_w, cat_w), 0) // DN_CHUNK
                == lax.broadcasted_iota(jnp.int32, (cat_w, cat_w), 1) // DN_CHUNK)
    key_blk = (lax.broadcasted_iota(jnp.int32, (tm, n_chunks * DN_HEAD_DIM), 0) // DN_CHUNK
               == lax.broadcasted_iota(jnp.int32, (tm, n_chunks * DN_HEAD_DIM), 1) // DN_HEAD_DIM)

    def block_diag(m_cat):
        tiled = jnp.concatenate([m_cat] * n_chunks, axis=0)
        return jnp.where(diag_blk, tiled, jnp.zeros_like(tiled))

    def chunk_cat(v):
        return jnp.concatenate([v[crows[c]] for c in chunks], axis=1)

    qe_l, ke_l, xin_l, gl_l, sc_l, decay_l = [], [], [], [], [], []
    for hd in heads:
        qh = qkv_ref[:, hd * DN_HEAD_DIM:(hd + 1) * DN_HEAD_DIM]
        kh = qkv_ref[:, DN_WIDTH + hd * DN_HEAD_DIM:DN_WIDTH + (hd + 1) * DN_HEAD_DIM]
        vh = qkv_ref[:, 2 * DN_WIDTH + hd * DN_HEAD_DIM:2 * DN_WIDTH + (hd + 1) * DN_HEAD_DIM]
        qh = qh * lax.rsqrt(jnp.sum(qh * qh, axis=-1, keepdims=True) + EPS) * (DN_HEAD_DIM ** -0.5)
        kh = kh * lax.rsqrt(jnp.sum(kh * kh, axis=-1, keepdims=True) + EPS)
        beta = beta_col[:, DN_HEADS + hd:DN_HEADS + hd + 1]
        gc = gc_col[:, hd:hd + 1]
        gl = gl_col[:, hd:hd + 1]
        kb = kh * beta
        eg = jnp.exp(gc)
        qe_l.append(qh * eg)
        ke_l.append(kh * jnp.exp(gl - gc))
        xin_l.append(jnp.concatenate([vh * beta, kb * eg], axis=1))
        gl_l.append(gl)
        kq = jnp.concatenate([chunk_cat(kb), chunk_cat(qh)], axis=0).astype(BF16)
        k_tiled = jnp.concatenate([kh.astype(BF16)] * n_chunks, axis=1)
        k_bd = jnp.where(key_blk, k_tiled, jnp.zeros_like(k_tiled))
        sc_l.append(_dot_nt(kq, k_bd))
        gc_i = gc[crows[n_chunks - 1]]
        for c in reversed(range(n_chunks - 1)):
            gc_i = jnp.where(cat_blk == c, gc[crows[c]], gc_i)
        decay_l.append(jnp.exp(jnp.where(lower, gc_i - gc_row[hd:hd + 1, :], NEG_INF)))

    pw_l = [jnp.where(strict, sc_l[hd][0:DN_CHUNK] * decay_l[hd], 0.0) for hd in heads]
    intra_l = [(sc_l[hd][DN_CHUNK:] * decay_l[hd]).astype(BF16) for hd in heads]
    q_l = [-pw_l[hd] for hd in heads]
    n_steps = int(np.log2(DN_CHUNK))
    for step in range(n_steps):
        bd_l = [block_diag(pw_l[hd].astype(BF16)) for hd in heads]
        if step == 0:
            pw_l = [_dot(pw_l[hd].astype(BF16), bd_l[hd]) for hd in heads]
            continue
        last = step == n_steps - 1
        lhs_l = [q_l[hd].astype(BF16) if last else
                 jnp.concatenate([pw_l[hd], q_l[hd]], axis=0).astype(BF16) for hd in heads]
        r_l = [_dot(lhs_l[hd], bd_l[hd]) for hd in heads]
        q_l = [q_l[hd] + pw_l[hd] + r_l[hd][-DN_CHUNK:] for hd in heads]
        if not last:
            pw_l = [r_l[hd][0:DN_CHUNK] for hd in heads]
    uw_l = [xin_l[hd] + _dot(block_diag(q_l[hd].astype(BF16)), xin_l[hd].astype(BF16)) for hd in heads]

    state = [state_ref[hd] for hd in heads]
    o_inter = [[] for _ in heads]
    v_news = [[] for _ in heads]
    for c in chunks:
        res = []
        for hd in heads:
            wq = jnp.concatenate([uw_l[hd][crows[c], DN_HEAD_DIM:], qe_l[hd][crows[c]]], axis=0).astype(BF16)
            res.append(_dot(wq, state[hd].astype(BF16)))
        for hd in heads:
            v_new = uw_l[hd][crows[c], 0:DN_HEAD_DIM] - res[hd][0:DN_CHUNK]
            v_news[hd].append(v_new)
            o_inter[hd].append(res[hd][DN_CHUNK:])
            g_last = gl_l[hd][c * DN_CHUNK:c * DN_CHUNK + 1, :]
            state[hd] = state[hd] * jnp.exp(g_last) + _dot(ke_l[hd][crows[c]].T.astype(BF16),
                                                           v_new.astype(BF16))
    dn_gain = vec(V_DN_GAIN, DN_HEAD_DIM)
    for hd in heads:
        state_ref[hd] = state[hd]
        v_new_all = jnp.concatenate(v_news[hd], axis=0).astype(BF16)
        od = jnp.concatenate(o_inter[hd], axis=0) + _dot(block_diag(intra_l[hd]), v_new_all)
        od = od * lax.rsqrt(jnp.mean(od * od, axis=-1, keepdims=True) + EPS) * dn_gain
        yc_ref[:, hd * DN_HEAD_DIM:(hd + 1) * DN_HEAD_DIM] = od

    yc = yc_ref[...] * _silu(proj(wtail_ref, TAIL_ZC, TAIL_MG))
    merged = merged + _dot(yc.astype(BF16), wpc_ref[...]) * gates_ref[:, 2 * D_MODEL:3 * D_MODEL]

    o_ref[...] = x + gate * _dot(merged.astype(BF16), wout_ref[...])


def _alibi_bias():
    qi = np.arange(WINDOW)[:, None]
    kj = np.arange(2 * WINDOW)[None, :]
    dist = qi + WINDOW - kj
    valid = (dist >= 0) & (dist < WINDOW)
    slopes = np.exp2(-8.0 * np.arange(1, ATT_HEADS + 1, dtype=np.float32) / ATT_HEADS)
    bias = np.where(valid[None], -slopes[:, None, None] * dist[None].astype(np.float32), NEG_INF)
    bias = bias.reshape(ATT_KV_HEADS, ATT_GROUP * WINDOW, 2 * WINDOW)
    return jnp.asarray(np.swapaxes(bias, 1, 2), dtype=F32)


def _layer_call(layer, x, mod, p):
    bsz, seq, d = x.shape
    tm = TILE
    assert seq % tm == 0 and d == D_MODEL and bsz <= SUBLANES

    def layer_spec(a):
        nd = a.ndim - 1
        return pl.BlockSpec((None,) + a.shape[1:], lambda b, t: (layer,) + (0,) * nd,
                            pipeline_mode=pl.Buffered(1))

    def const_spec(a):
        nd = a.ndim
        return pl.BlockSpec(a.shape, lambda b, t: (0,) * nd, pipeline_mode=pl.Buffered(1))

    def w_in_spec(col_block):
        return pl.BlockSpec((None, d, MAIN_COLS), lambda b, t: (layer, 0, col_block),
                            pipeline_mode=pl.Buffered(1))

    tile_spec = pl.BlockSpec((None, tm, d), lambda b, t: (b, t, 0))
    after_bias = [p["dw_w"], p["pw2_w"], p["sconv_w"], p["w_proj_a"], p["w_proj_b"], p["w_proj_c"],
                  p["w_out"]]
    return pl.pallas_call(
        _layer_body,
        grid=(bsz, seq // tm),
        in_specs=([tile_spec, layer_spec(mod), layer_spec(p["vecs"]), w_in_spec(0), w_in_spec(1),
                   layer_spec(p["sinks"]), const_spec(p["bias"])]
                  + [layer_spec(a) for a in after_bias]),
        out_specs=tile_spec,
        out_shape=jax.ShapeDtypeStruct(x.shape, x.dtype),
        scratch_shapes=[
            pltpu.VMEM((ATT_KV_HEADS, WINDOW, GROUP_WIDTH), F32),
            pltpu.VMEM((ATT_KV_WIDTH, WINDOW), BF16),
            pltpu.VMEM((CONV_PAD + tm, CONV_WIDTH), F32),
            pltpu.VMEM((SCONV_PAD + tm, 3 * DN_WIDTH), F32),
            pltpu.VMEM((DN_HEADS, DN_HEAD_DIM, DN_HEAD_DIM), F32),
            pltpu.VMEM((tm, ATT_WIDTH), F32),
            pltpu.VMEM((SUBLANES - 1, tm + CONV_PAD - SUBLANES, CONV_WIDTH), F32),
            pltpu.VMEM((tm, CONV_WIDTH), F32),
            pltpu.VMEM((tm, 3 * DN_WIDTH), F32),
            pltpu.VMEM((tm, DN_WIDTH), F32),
            pltpu.VMEM((tm, 3 * D_MODEL), F32),
            pltpu.VMEM((d, TAIL_COLS), BF16),
        ],
        compiler_params=pltpu.CompilerParams(
            dimension_semantics=("arbitrary", "arbitrary"),
            vmem_limit_bytes=VMEM_LIMIT),
        name="hybrid_layer",
    )(x, mod, p["vecs"], p["w_in"], p["w_in"], p["sinks"], p["bias"], *after_bias)


def _prep_params(norm_g, w_in, q_norm_g, k_norm_g, sinks, dw_w, dw_b, ln_g, ln_b, pw2_w, pw2_b,
                 sconv_w, a_log, dt_bias, dn_norm_g, w_proj_a, w_proj_b, w_proj_c, w_out):
    depth, d, _ = w_in.shape

    def row(v):
        return jnp.pad(v, ((0, 0), (0, D_MODEL - v.shape[1])))[:, None, :]

    vec_rows = [None] * N_VECS
    vec_rows[V_NORM_G] = row(norm_g)
    vec_rows[V_Q_GAIN] = row(jnp.tile(q_norm_g, (1, ATT_HEADS)))
    vec_rows[V_K_GAIN] = row(jnp.tile(k_norm_g, (1, ATT_KV_HEADS)))
    vec_rows[V_DW_B] = row(dw_b)
    vec_rows[V_LN_G] = row(ln_g)
    vec_rows[V_LN_B] = row(ln_b)
    vec_rows[V_PW2_B] = row(pw2_b)
    vec_rows[V_A_LOG] = row(a_log)
    vec_rows[V_DT_BIAS] = row(dt_bias)
    vec_rows[V_DN_GAIN] = row(dn_norm_g)
    return dict(
        vecs=jnp.stack(vec_rows, axis=1),
        w_in=w_in.astype(BF16),
        sinks=jnp.repeat(sinks, WINDOW, axis=1).reshape(depth, ATT_KV_HEADS, 1, ATT_GROUP * WINDOW),
        bias=_alibi_bias(),
        dw_w=jnp.broadcast_to(dw_w[:, :, None, :], (depth, CONV_K, SUBLANES, CONV_WIDTH)),
        pw2_w=pw2_w.astype(BF16),
        sconv_w=jnp.broadcast_to(sconv_w[:, :, None, :], (depth, DN_CONV_K, SUBLANES, 3 * DN_WIDTH)),
        w_proj_a=w_proj_a.astype(BF16),
        w_proj_b=w_proj_b.astype(BF16),
        w_proj_c=w_proj_c.astype(BF16),
        w_out=w_out.astype(BF16),
    )


@jax.jit
def _forward(x, c, w_ada, b_ada, norm_g, w_in, q_norm_g, k_norm_g, sinks, dw_w, dw_b, ln_g, ln_b,
             pw2_w, pw2_b, sconv_w, a_log, dt_bias, dn_norm_g, w_proj_a, w_proj_b, w_proj_c, w_out):
    bsz, _, d = x.shape
    depth = w_ada.shape[0]
    c_pad = jnp.zeros((SUBLANES, d), F32).at[:bsz].set(c)
    mod = _ada_call(c_pad, w_ada, b_ada)
    p = _prep_params(norm_g, w_in, q_norm_g, k_norm_g, sinks, dw_w, dw_b, ln_g, ln_b, pw2_w, pw2_b,
                     sconv_w, a_log, dt_bias, dn_norm_g, w_proj_a, w_proj_b, w_proj_c, w_out)
    for layer in range(depth):
        x = _layer_call(layer, x, mod, p)
    return x


def kernel(x, c, w_ada, b_ada, norm_g, w_in, q_norm_g, k_norm_g, sinks, dw_w, dw_b, ln_g, ln_b, pw2_w, pw2_b, sconv_w, a_log, dt_bias, dn_norm_g, w_proj_a, w_proj_b, w_proj_c, w_out):
    return _forward(x, c, w_ada, b_ada, norm_g, w_in, q_norm_g, k_norm_g, sinks, dw_w, dw_b, ln_g, ln_b,
                    pw2_w, pw2_b, sconv_w, a_log, dt_bias, dn_norm_g, w_proj_a, w_proj_b, w_proj_c, w_out)
```

```python
import numpy as np
import jax
import jax.numpy as jnp
from jax import lax
from jax.experimental import pallas as pl
from jax.experimental.pallas import tpu as pltpu

F32 = jnp.float32
BF16 = jnp.bfloat16

D_MODEL = 1024
ATT_HEADS = 8
ATT_KV_HEADS = 2
ATT_HEAD_DIM = 64
ATT_GROUP = ATT_HEADS // ATT_KV_HEADS
ATT_WIDTH = ATT_HEADS * ATT_HEAD_DIM
ATT_KV_WIDTH = ATT_KV_HEADS * ATT_HEAD_DIM
GROUP_WIDTH = ATT_GROUP * ATT_HEAD_DIM
WINDOW = 128
CONV_WIDTH = 512
CONV_K = 31
DN_HEADS = 4
DN_HEAD_DIM = 128
DN_WIDTH = DN_HEADS * DN_HEAD_DIM
DN_CONV_K = 4
DN_CHUNK = 64
EPS = 1e-6
NEG_INF = -1e30
LOG2_E = 1.4426950408889634

LANES = 128
SUBLANES = 8

TILE = 256
CONV_PAD = 32
SCONV_PAD = 8
CONV_ROWS = 32
GATE_COLS = 512
TAIL_SHIFT_COLS = 512
VMEM_LIMIT = 56 * 1024 * 1024

OFF_A = 0
OFF_B = OFF_A + 2 * ATT_WIDTH + 2 * ATT_KV_WIDTH
OFF_C = OFF_B + 3 * CONV_WIDTH
MAIN_COLS = OFF_C + 3 * DN_WIDTH
AB_COLS = 2 * DN_HEADS
TAIL_ZC = 0
TAIL_MG = TAIL_ZC + DN_WIDTH
TAIL_COLS = TAIL_MG + 3 * D_MODEL

(V_NORM_G, V_Q_GAIN, V_K_GAIN, V_DW_B, V_LN_G, V_LN_B, V_PW2_B, V_A_LOG, V_DT_BIAS, V_DN_GAIN,
 N_VECS) = range(11)


def _dot(a, b):
    return jnp.dot(a, b, preferred_element_type=F32)


def _dot_nt(a, b):
    return lax.dot_general(a, b, (((1,), (1,)), ((), ())), preferred_element_type=F32)


def _sigmoid(x):
    return 0.5 * jnp.tanh(0.5 * x) + 0.5


def _silu(x):
    h = 0.5 * x
    return h * jnp.tanh(h) + h


def _softplus(x):
    return jnp.maximum(x, 0.0) + jnp.log1p(jnp.exp(-jnp.abs(x)))


def _ada_body(c_ref, w_ref, b_ref, o_ref):
    sc = _silu(c_ref[...]).astype(BF16)
    o_ref[...] = _dot(sc, w_ref[...].astype(BF16)) + b_ref[...]


def _ada_call(c_pad, w_ada, b_ada):
    depth, d, d3 = w_ada.shape
    nblk = d3 // d
    return pl.pallas_call(
        _ada_body,
        grid=(depth, nblk),
        in_specs=[
            pl.BlockSpec((SUBLANES, d), lambda l, j: (0, 0)),
            pl.BlockSpec((None, d, d), lambda l, j: (l, 0, j)),
            pl.BlockSpec((None, 1, d), lambda l, j: (l, 0, j)),
        ],
        out_specs=pl.BlockSpec((None, SUBLANES, d), lambda l, j: (l, 0, j)),
        out_shape=jax.ShapeDtypeStruct((depth, SUBLANES, d3), F32),
        name="adaln_mod",
    )(c_pad, w_ada, b_ada.reshape(depth, 1, d3))


def _layer_body(x_ref, mod_ref, vec_ref, wmain_ref, wrest_ref, sink_ref, bias_ref,
                dww_ref, pw2w_ref, scw_ref, wpa_ref, wpb_ref, wpc_ref, wout_ref,
                o_ref,
                kprev_ref, vprev_ref, convbuf_ref, sconvbuf_ref, state_ref,
                ya_ref, phase_ref, cv_ref, qkv_ref, yc_ref, gates_ref, wtail_ref):
    tm = x_ref.shape[0]
    t = pl.program_id(1)

    @pl.when((t == 0) & (pl.program_id(0) == 0))
    def _align_tail_weights():
        for lo in range(0, TAIL_COLS, TAIL_SHIFT_COLS):
            window = wrest_ref[:, lo:lo + TAIL_SHIFT_COLS + LANES].astype(F32)
            moved = pltpu.roll(window, TAIL_SHIFT_COLS + LANES - AB_COLS, axis=1)
            wtail_ref[:, lo:lo + TAIL_SHIFT_COLS] = moved[:, 0:TAIL_SHIFT_COLS].astype(BF16)

    @pl.when(t == 0)
    def _reset():
        kprev_ref[...] = jnp.zeros_like(kprev_ref)
        vprev_ref[...] = jnp.zeros_like(vprev_ref)
        convbuf_ref[0:CONV_PAD, :] = jnp.zeros((CONV_PAD, CONV_WIDTH), F32)
        sconvbuf_ref[0:SCONV_PAD, :] = jnp.zeros((SCONV_PAD, 3 * DN_WIDTH), F32)
        state_ref[...] = jnp.zeros_like(state_ref)

    def vec(row, width):
        return vec_ref[row, :, 0:width]

    mod = mod_ref[pl.ds(pl.program_id(0), 1), :]
    shift = mod[:, 0:D_MODEL]
    scale = mod[:, D_MODEL:2 * D_MODEL]
    gate = mod[:, 2 * D_MODEL:3 * D_MODEL]
    x = x_ref[...]
    ms = jnp.mean(x * x, axis=-1, keepdims=True)
    h = x * lax.rsqrt(ms + EPS) * (vec(V_NORM_G, D_MODEL) * (1.0 + scale)) + shift
    hb = h.astype(BF16)

    def proj(w_ref, lo, hi):
        return _dot(hb, w_ref[:, lo:hi])

    pa = proj(wmain_ref, OFF_A, OFF_B)
    qa = pa[:, 0:ATT_WIDTH]
    ka = pa[:, ATT_WIDTH:ATT_WIDTH + ATT_KV_WIDTH]
    va = pa[:, ATT_WIDTH + ATT_KV_WIDTH:ATT_WIDTH + 2 * ATT_KV_WIDTH]
    za = pa[:, ATT_WIDTH + 2 * ATT_KV_WIDTH:]

    def head_mean_square(v, width):
        r = lax.broadcasted_iota(jnp.int32, (width, width), 0) // ATT_HEAD_DIM
        c = lax.broadcasted_iota(jnp.int32, (width, width), 1) // ATT_HEAD_DIM
        avg = jnp.where(r == c, 1.0 / ATT_HEAD_DIM, 0.0).astype(BF16)
        sq = v * v
        hi = sq.astype(BF16)
        lo = (sq - hi.astype(F32)).astype(BF16)
        return _dot(hi, avg) + _dot(lo, avg)

    qn = qa * lax.rsqrt(head_mean_square(qa, ATT_WIDTH) + EPS) * vec(V_Q_GAIN, ATT_WIDTH)
    kn = ka * lax.rsqrt(head_mean_square(ka, ATT_KV_WIDTH) + EPS) * vec(V_K_GAIN, ATT_KV_WIDTH)

    low_half = lax.broadcasted_iota(jnp.int32, (tm, ATT_KV_WIDTH), 1) < ATT_HEAD_DIM
    k_swapped = pltpu.roll(kn, ATT_HEAD_DIM, axis=1)
    krep = [jnp.concatenate([kh, kh], axis=1).astype(BF16)
            for kh in (jnp.where(low_half, kn, k_swapped), jnp.where(low_half, k_swapped, kn))]
    v_t = va.T.astype(BF16)

    q_head = lax.broadcasted_iota(jnp.int32, (WINDOW, GROUP_WIDTH), 1) // ATT_HEAD_DIM
    key_row = lax.broadcasted_iota(jnp.int32, (2 * WINDOW, ATT_GROUP * WINDOW), 0)
    n_masked_rows = jnp.where(t == 0, WINDOW, 0)

    blocks = [(j, g) for j in range(tm // WINDOW) for g in range(ATT_KV_HEADS)]
    scores, v_cats = {}, {}
    for j, g in blocks:
        rows = slice(j * WINDOW, (j + 1) * WINDOW)
        if j == 0:
            k_prev, v_prev = kprev_ref[g].astype(BF16), vprev_ref[...]
        else:
            prev_rows = slice((j - 1) * WINDOW, j * WINDOW)
            k_prev, v_prev = krep[g][prev_rows], v_t[:, prev_rows]
        kcat = jnp.concatenate([k_prev, krep[g][rows]], axis=0)
        v_cats[j] = jnp.concatenate([v_prev, v_t[:, rows]], axis=1)
        qg = qn[rows, g * GROUP_WIDTH:(g + 1) * GROUP_WIDTH]
        qstack = jnp.concatenate(
            [jnp.where(q_head == i, qg, 0.0) for i in range(ATT_GROUP)], axis=0).astype(BF16)
        s = _dot_nt(kcat, qstack) + bias_ref[g]
        if j == 0:
            s = jnp.where(key_row < n_masked_rows, NEG_INF, s)
        scores[j, g] = s
    probs = {}
    for j, g in blocks:
        s = scores[j, g]
        sink = sink_ref[g]
        m = jnp.maximum(jnp.max(s, axis=0, keepdims=True), sink)
        p = jnp.exp2(s - m)
        denom = jnp.sum(p, axis=0, keepdims=True) + jnp.exp2(sink - m)
        probs[j, g] = (p * (1.0 / denom)).astype(BF16)
    for j, g in blocks:
        o_t = _dot(v_cats[j][g * ATT_HEAD_DIM:(g + 1) * ATT_HEAD_DIM], probs[j, g])
        stacked = jnp.concatenate([o_t[:, i * WINDOW:(i + 1) * WINDOW] for i in range(ATT_GROUP)], axis=0)
        ya_ref[j * WINDOW:(j + 1) * WINDOW, g * GROUP_WIDTH:(g + 1) * GROUP_WIDTH] = stacked.T

    for g in range(ATT_KV_HEADS):
        kprev_ref[g] = krep[g][tm - WINDOW:tm].astype(F32)
    vprev_ref[...] = v_t[:, tm - WINDOW:tm]

    ya = ya_ref[...] * _silu(za)

    pb = proj(wmain_ref, OFF_B, OFF_C)
    ub = pb[:, 0:CONV_WIDTH] * _sigmoid(pb[:, CONV_WIDTH:2 * CONV_WIDTH])
    zb = pb[:, 2 * CONV_WIDTH:]
    convbuf_ref[CONV_PAD:CONV_PAD + tm, :] = ub
    base = CONV_PAD - (CONV_K - 1)
    phase_rows = phase_ref.shape[1]
    for ph in range(1, SUBLANES):
        phase_ref[ph - 1] = convbuf_ref[ph:ph + phase_rows, :]
    sconvbuf_ref[SCONV_PAD:SCONV_PAD + tm, :] = proj(wmain_ref, OFF_C, MAIN_COLS)
    for piece in range(3 * D_MODEL // GATE_COLS):
        cols = slice(piece * GATE_COLS, (piece + 1) * GATE_COLS)
        gates_ref[:, cols] = _sigmoid(
            proj(wtail_ref, TAIL_MG + piece * GATE_COLS, TAIL_MG + (piece + 1) * GATE_COLS))
    dw_b = vec(V_DW_B, CONV_WIDTH)
    for r in range(tm // CONV_ROWS):
        acc = jnp.broadcast_to(dw_b, (CONV_ROWS, CONV_WIDTH))
        for k in range(CONV_K):
            ph = (base + k) % SUBLANES
            start = (base + k) - ph + r * CONV_ROWS
            if ph == 0:
                win = convbuf_ref[start:start + CONV_ROWS, :]
            else:
                win = phase_ref[ph - 1, start:start + CONV_ROWS, :]
            acc = acc + jnp.concatenate([dww_ref[k]] * (CONV_ROWS // SUBLANES), axis=0) * win
        cv_ref[r * CONV_ROWS:(r + 1) * CONV_ROWS, :] = acc
    convbuf_ref[0:CONV_PAD, :] = convbuf_ref[tm:tm + CONV_PAD, :]
    cv = cv_ref[...]
    mu = jnp.mean(cv, axis=-1, keepdims=True)
    xc = cv - mu
    var = jnp.mean(xc * xc, axis=-1, keepdims=True)
    ln = xc * lax.rsqrt(var + EPS) * vec(V_LN_G, CONV_WIDTH) + vec(V_LN_B, CONV_WIDTH)
    yb = (_dot(_silu(ln).astype(BF16), pw2w_ref[...]) + vec(V_PW2_B, CONV_WIDTH)) * _silu(zb)
    merged = _dot(ya.astype(BF16), wpa_ref[...]) * gates_ref[:, 0:D_MODEL]
    merged = merged + _dot(yb.astype(BF16), wpb_ref[...]) * gates_ref[:, D_MODEL:2 * D_MODEL]

    sbase = SCONV_PAD - (DN_CONV_K - 1)
    for r in range(tm // CONV_ROWS):
        for cblk in range(3):
            cols = slice(cblk * DN_WIDTH, (cblk + 1) * DN_WIDTH)
            acc = None
            for k in range(DN_CONV_K):
                start = sbase + k + r * CONV_ROWS
                tap = jnp.concatenate([scw_ref[k, :, cols]] * (CONV_ROWS // SUBLANES), axis=0)
                term = tap * sconvbuf_ref[start:start + CONV_ROWS, cols]
                acc = term if acc is None else acc + term
            qkv_ref[r * CONV_ROWS:(r + 1) * CONV_ROWS, cols] = _silu(acc)
    sconvbuf_ref[0:SCONV_PAD, :] = sconvbuf_ref[tm:tm + SCONV_PAD, :]

    ab = _dot(hb, wrest_ref[:, 0:LANES])
    g_col = -jnp.exp(vec(V_A_LOG, LANES)) * _softplus(ab + vec(V_DT_BIAS, LANES))
    beta_col = _sigmoid(ab)

    n_chunks = tm // DN_CHUNK
    ri = lax.broadcasted_iota(jnp.int32, (tm, tm), 0)
    ci = lax.broadcasted_iota(jnp.int32, (tm, tm), 1)
    same_chunk = (ri // DN_CHUNK) == (ci // DN_CHUNK)
    g_hi = g_col.astype(BF16)
    g_mid = (g_col - g_hi.astype(F32)).astype(BF16)
    g_lo = (g_col - g_hi.astype(F32) - g_mid.astype(F32)).astype(BF16)
    tri = jnp.where(same_chunk & (ri >= ci), 1.0, 0.0).astype(BF16)
    ones = jnp.where(same_chunk, 1.0, 0.0).astype(BF16)
    gc_col = _dot(tri, g_hi) + _dot(tri, g_mid) + _dot(tri, g_lo)
    gl_col = _dot(ones, g_hi) + _dot(ones, g_mid) + _dot(ones, g_lo)
    gc_row = gc_col.T

    heads = range(DN_HEADS)
    chunks = range(n_chunks)
    crows = [slice(c * DN_CHUNK, (c + 1) * DN_CHUNK) for c in chunks]
    cat_w = n_chunks * DN_CHUNK
    cat_row = lax.broadcasted_iota(jnp.int32, (DN_CHUNK, cat_w), 0)
    cat_lane = lax.broadcasted_iota(jnp.int32, (DN_CHUNK, cat_w), 1)
    cat_blk = cat_lane // DN_CHUNK
    cat_col = cat_lane - cat_blk * DN_CHUNK
    lower = cat_row >= cat_col
    strict = cat_row > cat_col
    diag_blk = (lax.broadcasted_iota(jnp.int32, (cat_w, cat_w), 0) // DN_CHUNK
                == lax.broadcasted_iota(jnp.int32, (cat_w, cat_w), 1) // DN_CHUNK)
    key_blk = (lax.broadcasted_iota(jnp.int32, (tm, n_chunks * DN_HEAD_DIM), 0) // DN_CHUNK
               == lax.broadcasted_iota(jnp.int32, (tm, n_chunks * DN_HEAD_DIM), 1) // DN_HEAD_DIM)

    def block_diag(m_cat):
        tiled = jnp.concatenate([m_cat] * n_chunks, axis=0)
        return jnp.where(diag_blk, tiled, jnp.zeros_like(tiled))

    def chunk_cat(v):
        return jnp.concatenate([v[crows[c]] for c in chunks], axis=1)

    qe_l, ke_l, xin_l, gl_l, sc_l, decay_l = [], [], [], [], [], []
    for hd in heads:
        qh = qkv_ref[:, hd * DN_HEAD_DIM:(hd + 1) * DN_HEAD_DIM]
        kh = qkv_ref[:, DN_WIDTH + hd * DN_HEAD_DIM:DN_WIDTH + (hd + 1) * DN_HEAD_DIM]
        vh = qkv_ref[:, 2 * DN_WIDTH + hd * DN_HEAD_DIM:2 * DN_WIDTH + (hd + 1) * DN_HEAD_DIM]
        qh = qh * lax.rsqrt(jnp.sum(qh * qh, axis=-1, keepdims=True) + EPS) * (DN_HEAD_DIM ** -0.5)
        kh = kh * lax.rsqrt(jnp.sum(kh * kh, axis=-1, keepdims=True) + EPS)
        beta = beta_col[:, DN_HEADS + hd:DN_HEADS + hd + 1]
        gc = gc_col[:, hd:hd + 1]
        gl = gl_col[:, hd:hd + 1]
        kb = kh * beta
        eg = jnp.exp(gc)
        qe_l.append(qh * eg)
        ke_l.append(kh * jnp.exp(gl - gc))
        xin_l.append(jnp.concatenate([vh * beta, kb * eg], axis=1))
        gl_l.append(gl)
        kq = jnp.concatenate([chunk_cat(kb), chunk_cat(qh)], axis=0).astype(BF16)
        k_tiled = jnp.concatenate([kh.astype(BF16)] * n_chunks, axis=1)
        k_bd = jnp.where(key_blk, k_tiled, jnp.zeros_like(k_tiled))
        sc_l.append(_dot_nt(kq, k_bd))
        gc_i = gc[crows[n_chunks - 1]]
        for c in reversed(range(n_chunks - 1)):
            gc_i = jnp.where(cat_blk == c, gc[crows[c]], gc_i)
        decay_l.append(jnp.exp(jnp.where(lower, gc_i - gc_row[hd:hd + 1, :], NEG_INF)))

    pw_l = [jnp.where(strict, sc_l[hd][0:DN_CHUNK] * decay_l[hd], 0.0) for hd in heads]
    intra_l = [(sc_l[hd][DN_CHUNK:] * decay_l[hd]).astype(BF16) for hd in heads]
    q_l = [-pw_l[hd] for hd in heads]
    n_steps = int(np.log2(DN_CHUNK))
    for step in range(n_steps):
        bd_l = [block_diag(pw_l[hd].astype(BF16)) for hd in heads]
        if step == 0:
            pw_l = [_dot(pw_l[hd].astype(BF16), bd_l[hd]) for hd in heads]
            continue
        last = step == n_steps - 1
        lhs_l = [q_l[hd].astype(BF16) if last else
                 jnp.concatenate([pw_l[hd], q_l[hd]], axis=0).astype(BF16) for hd in heads]
        r_l = [_dot(lhs_l[hd], bd_l[hd]) for hd in heads]
        q_l = [q_l[hd] + pw_l[hd] + r_l[hd][-DN_CHUNK:] for hd in heads]
        if not last:
            pw_l = [r_l[hd][0:DN_CHUNK] for hd in heads]
    uw_l = [xin_l[hd] + _dot(block_diag(q_l[hd].astype(BF16)), xin_l[hd].astype(BF16)) for hd in heads]

    state = [state_ref[hd] for hd in heads]
    o_inter = [[] for _ in heads]
    v_news = [[] for _ in heads]
    for c in chunks:
        res = []
        for hd in heads:
            wq = jnp.concatenate([uw_l[hd][crows[c], DN_HEAD_DIM:], qe_l[hd][crows[c]]], axis=0).astype(BF16)
            res.append(_dot(wq, state[hd].astype(BF16)))
        for hd in heads:
            v_new = uw_l[hd][crows[c], 0:DN_HEAD_DIM] - res[hd][0:DN_CHUNK]
            v_news[hd].append(v_new)
            o_inter[hd].append(res[hd][DN_CHUNK:])
            g_last = gl_l[hd][c * DN_CHUNK:c * DN_CHUNK + 1, :]
            state[hd] = state[hd] * jnp.exp(g_last) + _dot(ke_l[hd][crows[c]].T.astype(BF16),
                                                           v_new.astype(BF16))
    dn_gain = vec(V_DN_GAIN, DN_HEAD_DIM)
    for hd in heads:
        state_ref[hd] = state[hd]
        v_new_all = jnp.concatenate(v_news[hd], axis=0).astype(BF16)
        od = jnp.concatenate(o_inter[hd], axis=0) + _dot(block_diag(intra_l[hd]), v_new_all)
        od = od * lax.rsqrt(jnp.mean(od * od, axis=-1, keepdims=True) + EPS) * dn_gain
        yc_ref[:, hd * DN_HEAD_DIM:(hd + 1) * DN_HEAD_DIM] = od

    yc = yc_ref[...] * _silu(proj(wtail_ref, TAIL_ZC, TAIL_MG))
    merged = merged + _dot(yc.astype(BF16), wpc_ref[...]) * gates_ref[:, 2 * D_MODEL:3 * D_MODEL]

    o_ref[...] = x + gate * _dot(merged.astype(BF16), wout_ref[...])


def _alibi_bias():
    qi = np.arange(WINDOW)[:, None]
    kj = np.arange(2 * WINDOW)[None, :]
    dist = qi + WINDOW - kj
    valid = (dist >= 0) & (dist < WINDOW)
    slopes = np.exp2(-8.0 * np.arange(1, ATT_HEADS + 1, dtype=np.float32) / ATT_HEADS)
    bias = np.where(valid[None], -LOG2_E * slopes[:, None, None] * dist[None].astype(np.float32), NEG_INF)
    bias = bias.reshape(ATT_KV_HEADS, ATT_GROUP * WINDOW, 2 * WINDOW)
    return jnp.asarray(np.swapaxes(bias, 1, 2), dtype=F32)


def _layer_call(layer, x, mod, p):
    bsz, seq, d = x.shape
    tm = TILE
    assert seq % tm == 0 and d == D_MODEL and bsz <= SUBLANES

    def layer_spec(a):
        nd = a.ndim - 1
        return pl.BlockSpec((None,) + a.shape[1:], lambda b, t: (layer,) + (0,) * nd,
                            pipeline_mode=pl.Buffered(1))

    def const_spec(a):
        nd = a.ndim
        return pl.BlockSpec(a.shape, lambda b, t: (0,) * nd, pipeline_mode=pl.Buffered(1))

    def w_in_spec(col_block):
        return pl.BlockSpec((None, d, MAIN_COLS), lambda b, t: (layer, 0, col_block),
                            pipeline_mode=pl.Buffered(1))

    tile_spec = pl.BlockSpec((None, tm, d), lambda b, t: (b, t, 0))
    after_bias = [p["dw_w"], p["pw2_w"], p["sconv_w"], p["w_proj_a"], p["w_proj_b"], p["w_proj_c"],
                  p["w_out"]]
    return pl.pallas_call(
        _layer_body,
        grid=(bsz, seq // tm),
        in_specs=([tile_spec, layer_spec(mod), layer_spec(p["vecs"]), w_in_spec(0), w_in_spec(1),
                   layer_spec(p["sinks"]), const_spec(p["bias"])]
                  + [layer_spec(a) for a in after_bias]),
        out_specs=tile_spec,
        out_shape=jax.ShapeDtypeStruct(x.shape, x.dtype),
        scratch_shapes=[
            pltpu.VMEM((ATT_KV_HEADS, WINDOW, GROUP_WIDTH), F32),
            pltpu.VMEM((ATT_KV_WIDTH, WINDOW), BF16),
            pltpu.VMEM((CONV_PAD + tm, CONV_WIDTH), F32),
            pltpu.VMEM((SCONV_PAD + tm, 3 * DN_WIDTH), F32),
            pltpu.VMEM((DN_HEADS, DN_HEAD_DIM, DN_HEAD_DIM), F32),
            pltpu.VMEM((tm, ATT_WIDTH), F32),
            pltpu.VMEM((SUBLANES - 1, tm + CONV_PAD - SUBLANES, CONV_WIDTH), F32),
            pltpu.VMEM((tm, CONV_WIDTH), F32),
            pltpu.VMEM((tm, 3 * DN_WIDTH), F32),
            pltpu.VMEM((tm, DN_WIDTH), F32),
            pltpu.VMEM((tm, 3 * D_MODEL), F32),
            pltpu.VMEM((d, TAIL_COLS), BF16),
        ],
        compiler_params=pltpu.CompilerParams(
            dimension_semantics=("arbitrary", "arbitrary"),
            vmem_limit_bytes=VMEM_LIMIT),
        name="hybrid_layer",
    )(x, mod, p["vecs"], p["w_in"], p["w_in"], p["sinks"], p["bias"], *after_bias)


def _prep_params(norm_g, w_in, q_norm_g, k_norm_g, sinks, dw_w, dw_b, ln_g, ln_b, pw2_w, pw2_b,
                 sconv_w, a_log, dt_bias, dn_norm_g, w_proj_a, w_proj_b, w_proj_c, w_out):
    depth, d, _ = w_in.shape

    def row(v):
        return jnp.pad(v, ((0, 0), (0, D_MODEL - v.shape[1])))[:, None, :]

    vec_rows = [None] * N_VECS
    vec_rows[V_NORM_G] = row(norm_g)
    vec_rows[V_Q_GAIN] = row(jnp.tile(q_norm_g, (1, ATT_HEADS)) * (ATT_HEAD_DIM ** -0.5 * LOG2_E))
    vec_rows[V_K_GAIN] = row(jnp.tile(k_norm_g, (1, ATT_KV_HEADS)))
    vec_rows[V_DW_B] = row(dw_b)
    vec_rows[V_LN_G] = row(ln_g)
    vec_rows[V_LN_B] = row(ln_b)
    vec_rows[V_PW2_B] = row(pw2_b)
    vec_rows[V_A_LOG] = row(a_log)
    vec_rows[V_DT_BIAS] = row(dt_bias)
    vec_rows[V_DN_GAIN] = row(dn_norm_g)
    return dict(
        vecs=jnp.stack(vec_rows, axis=1),
        w_in=w_in.astype(BF16),
        sinks=jnp.repeat(sinks * LOG2_E, WINDOW, axis=1).reshape(depth, ATT_KV_HEADS, 1, ATT_GROUP * WINDOW),
        bias=_alibi_bias(),
        dw_w=jnp.broadcast_to(dw_w[:, :, None, :], (depth, CONV_K, SUBLANES, CONV_WIDTH)),
        pw2_w=pw2_w.astype(BF16),
        sconv_w=jnp.broadcast_to(sconv_w[:, :, None, :], (depth, DN_CONV_K, SUBLANES, 3 * DN_WIDTH)),
        w_proj_a=w_proj_a.astype(BF16),
        w_proj_b=w_proj_b.astype(BF16),
        w_proj_c=w_proj_c.astype(BF16),
        w_out=w_out.astype(BF16),
    )


@jax.jit
def _forward(x, c, w_ada, b_ada, norm_g, w_in, q_norm_g, k_norm_g, sinks, dw_w, dw_b, ln_g, ln_b,
             pw2_w, pw2_b, sconv_w, a_log, dt_bias, dn_norm_g, w_proj_a, w_proj_b, w_proj_c, w_out):
    bsz, _, d = x.shape
    depth = w_ada.shape[0]
    c_pad = jnp.zeros((SUBLANES, d), F32).at[:bsz].set(c)
    mod = _ada_call(c_pad, w_ada, b_ada)
    p = _prep_params(norm_g, w_in, q_norm_g, k_norm_g, sinks, dw_w, dw_b, ln_g, ln_b, pw2_w, pw2_b,
                     sconv_w, a_log, dt_bias, dn_norm_g, w_proj_a, w_proj_b, w_proj_c, w_out)
    for layer in range(depth):
        x = _layer_call(layer, x, mod, p)
    return x


def kernel(x, c, w_ada, b_ada, norm_g, w_in, q_norm_g, k_norm_g, sinks, dw_w, dw_b, ln_g, ln_b, pw2_w, pw2_b, sconv_w, a_log, dt_bias, dn_norm_g, w_proj_a, w_proj_b, w_proj_c, w_out):
    return _forward(x, c, w_ada, b_ada, norm_g, w_in, q_norm_g, k_norm_g, sinks, dw_w, dw_b, ln_g, ln_b,
                    pw2_w, pw2_b, sconv_w, a_log, dt_bias, dn_norm_g, w_proj_a, w_proj_b, w_proj_c, w_out)
```

```python
import numpy as np
import jax
import jax.numpy as jnp
from jax import lax
from jax.experimental import pallas as pl
from jax.experimental.pallas import tpu as pltpu

F32 = jnp.float32
BF16 = jnp.bfloat16

D_MODEL = 1024
ATT_HEADS = 8
ATT_KV_HEADS = 2
ATT_HEAD_DIM = 64
ATT_GROUP = ATT_HEADS // ATT_KV_HEADS
ATT_WIDTH = ATT_HEADS * ATT_HEAD_DIM
ATT_KV_WIDTH = ATT_KV_HEADS * ATT_HEAD_DIM
GROUP_WIDTH = ATT_GROUP * ATT_HEAD_DIM
WINDOW = 128
CONV_WIDTH = 512
CONV_K = 31
DN_HEADS = 4
DN_HEAD_DIM = 128
DN_WIDTH = DN_HEADS * DN_HEAD_DIM
DN_CONV_K = 4
DN_CHUNK = 64
EPS = 1e-6
NEG_INF = -1e30
LOG2_E = 1.4426950408889634

LANES = 128
SUBLANES = 8

TILE = 256
CONV_PAD = 32
SCONV_PAD = 8
CONV_ROWS = 32
GATE_COLS = 512
CAST_COLS = 1024
TAIL_SHIFT_COLS = 512
VMEM_LIMIT = 56 * 1024 * 1024

OFF_A = 0
OFF_B = OFF_A + 2 * ATT_WIDTH + 2 * ATT_KV_WIDTH
OFF_C = OFF_B + 3 * CONV_WIDTH
MAIN_COLS = OFF_C + 3 * DN_WIDTH
AB_COLS = 2 * DN_HEADS
TAIL_ZC = 0
TAIL_MG = TAIL_ZC + DN_WIDTH
TAIL_COLS = TAIL_MG + 3 * D_MODEL

(V_NORM_G, V_Q_GAIN, V_K_GAIN, V_DW_B, V_LN_G, V_LN_B, V_PW2_B, V_A_LOG, V_DT_BIAS, V_DN_GAIN,
 N_VECS) = range(11)


def _dot(a, b):
    return jnp.dot(a, b, preferred_element_type=F32)


def _dot_nt(a, b):
    return lax.dot_general(a, b, (((1,), (1,)), ((), ())), preferred_element_type=F32)


def _sigmoid(x):
    return 0.5 * jnp.tanh(0.5 * x) + 0.5


def _silu(x):
    h = 0.5 * x
    return h * jnp.tanh(h) + h


def _softplus(x):
    return jnp.maximum(x, 0.0) + jnp.log1p(jnp.exp(-jnp.abs(x)))


def _ada_body(c_ref, w_ref, b_ref, o_ref):
    sc = _silu(c_ref[...]).astype(BF16)
    o_ref[...] = _dot(sc, w_ref[...].astype(BF16)) + b_ref[...]


def _ada_call(c_pad, w_ada, b_ada):
    depth, d, d3 = w_ada.shape
    nblk = d3 // d
    return pl.pallas_call(
        _ada_body,
        grid=(depth, nblk),
        in_specs=[
            pl.BlockSpec((SUBLANES, d), lambda l, j: (0, 0)),
            pl.BlockSpec((None, d, d), lambda l, j: (l, 0, j)),
            pl.BlockSpec((None, 1, d), lambda l, j: (l, 0, j)),
        ],
        out_specs=pl.BlockSpec((None, SUBLANES, d), lambda l, j: (l, 0, j)),
        out_shape=jax.ShapeDtypeStruct((depth, SUBLANES, d3), F32),
        name="adaln_mod",
    )(c_pad, w_ada, b_ada.reshape(depth, 1, d3))


def _cast_body(w_ref, o_ref):
    o_ref[...] = w_ref[...].astype(BF16)


def _cast_call(w_in):
    depth, d, cols = w_in.shape
    return pl.pallas_call(
        _cast_body,
        grid=(depth, pl.cdiv(cols, CAST_COLS)),
        in_specs=[pl.BlockSpec((None, d, CAST_COLS), lambda l, j: (l, 0, j))],
        out_specs=pl.BlockSpec((None, d, CAST_COLS), lambda l, j: (l, 0, j)),
        out_shape=jax.ShapeDtypeStruct(w_in.shape, BF16),
        name="cast_w_in",
    )(w_in)


def _layer_body(x_ref, mod_ref, vec_ref, wmain_ref, wrest_ref, sink_ref, bias_ref,
                dww_ref, pw2w_ref, scw_ref, wpa_ref, wpb_ref, wpc_ref, wout_ref,
                o_ref,
                kprev_ref, vprev_ref, convbuf_ref, sconvbuf_ref, state_ref,
                ya_ref, phase_ref, cv_ref, qkv_ref, yc_ref, gates_ref, wtail_ref):
    tm = x_ref.shape[0]
    t = pl.program_id(1)

    @pl.when((t == 0) & (pl.program_id(0) == 0))
    def _align_tail_weights():
        for lo in range(0, TAIL_COLS, TAIL_SHIFT_COLS):
            window = wrest_ref[:, lo:lo + TAIL_SHIFT_COLS + LANES].astype(F32)
            moved = pltpu.roll(window, TAIL_SHIFT_COLS + LANES - AB_COLS, axis=1)
            wtail_ref[:, lo:lo + TAIL_SHIFT_COLS] = moved[:, 0:TAIL_SHIFT_COLS].astype(BF16)

    @pl.when(t == 0)
    def _reset():
        kprev_ref[...] = jnp.zeros_like(kprev_ref)
        vprev_ref[...] = jnp.zeros_like(vprev_ref)
        convbuf_ref[0:CONV_PAD, :] = jnp.zeros((CONV_PAD, CONV_WIDTH), F32)
        sconvbuf_ref[0:SCONV_PAD, :] = jnp.zeros((SCONV_PAD, 3 * DN_WIDTH), F32)
        state_ref[...] = jnp.zeros_like(state_ref)

    def vec(row, width):
        return vec_ref[row, :, 0:width]

    mod = mod_ref[pl.ds(pl.program_id(0), 1), :]
    shift = mod[:, 0:D_MODEL]
    scale = mod[:, D_MODEL:2 * D_MODEL]
    gate = mod[:, 2 * D_MODEL:3 * D_MODEL]
    x = x_ref[...]
    ms = jnp.mean(x * x, axis=-1, keepdims=True)
    h = x * lax.rsqrt(ms + EPS) * (vec(V_NORM_G, D_MODEL) * (1.0 + scale)) + shift
    hb = h.astype(BF16)

    def proj(w_ref, lo, hi):
        return _dot(hb, w_ref[:, lo:hi])

    pa = proj(wmain_ref, OFF_A, OFF_B)
    qa = pa[:, 0:ATT_WIDTH]
    ka = pa[:, ATT_WIDTH:ATT_WIDTH + ATT_KV_WIDTH]
    va = pa[:, ATT_WIDTH + ATT_KV_WIDTH:ATT_WIDTH + 2 * ATT_KV_WIDTH]
    za = pa[:, ATT_WIDTH + 2 * ATT_KV_WIDTH:]

    def head_mean_square(v, width):
        r = lax.broadcasted_iota(jnp.int32, (width, width), 0) // ATT_HEAD_DIM
        c = lax.broadcasted_iota(jnp.int32, (width, width), 1) // ATT_HEAD_DIM
        avg = jnp.where(r == c, 1.0 / ATT_HEAD_DIM, 0.0).astype(BF16)
        sq = v * v
        hi = sq.astype(BF16)
        lo = (sq - hi.astype(F32)).astype(BF16)
        return _dot(hi, avg) + _dot(lo, avg)

    qn = qa * lax.rsqrt(head_mean_square(qa, ATT_WIDTH) + EPS) * vec(V_Q_GAIN, ATT_WIDTH)
    kn = ka * lax.rsqrt(head_mean_square(ka, ATT_KV_WIDTH) + EPS) * vec(V_K_GAIN, ATT_KV_WIDTH)

    low_half = lax.broadcasted_iota(jnp.int32, (tm, ATT_KV_WIDTH), 1) < ATT_HEAD_DIM
    k_swapped = pltpu.roll(kn, ATT_HEAD_DIM, axis=1)
    krep = [jnp.concatenate([kh, kh], axis=1).astype(BF16)
            for kh in (jnp.where(low_half, kn, k_swapped), jnp.where(low_half, k_swapped, kn))]
    v_t = va.T.astype(BF16)

    q_head = lax.broadcasted_iota(jnp.int32, (WINDOW, GROUP_WIDTH), 1) // ATT_HEAD_DIM
    key_row = lax.broadcasted_iota(jnp.int32, (2 * WINDOW, ATT_GROUP * WINDOW), 0)
    n_masked_rows = jnp.where(t == 0, WINDOW, 0)

    blocks = [(j, g) for j in range(tm // WINDOW) for g in range(ATT_KV_HEADS)]
    scores, v_cats = {}, {}
    for j, g in blocks:
        rows = slice(j * WINDOW, (j + 1) * WINDOW)
        if j == 0:
            k_prev, v_prev = kprev_ref[g].astype(BF16), vprev_ref[...]
        else:
            prev_rows = slice((j - 1) * WINDOW, j * WINDOW)
            k_prev, v_prev = krep[g][prev_rows], v_t[:, prev_rows]
        kcat = jnp.concatenate([k_prev, krep[g][rows]], axis=0)
        v_cats[j] = jnp.concatenate([v_prev, v_t[:, rows]], axis=1)
        qg = qn[rows, g * GROUP_WIDTH:(g + 1) * GROUP_WIDTH]
        qstack = jnp.concatenate(
            [jnp.where(q_head == i, qg, 0.0) for i in range(ATT_GROUP)], axis=0).astype(BF16)
        s = _dot_nt(kcat, qstack) + bias_ref[g]
        if j == 0:
            s = jnp.where(key_row < n_masked_rows, NEG_INF, s)
        scores[j, g] = s
    probs = {}
    for j, g in blocks:
        s = scores[j, g]
        sink = sink_ref[g]
        m = jnp.maximum(jnp.max(s, axis=0, keepdims=True), sink)
        p = jnp.exp2(s - m)
        denom = jnp.sum(p, axis=0, keepdims=True) + jnp.exp2(sink - m)
        probs[j, g] = (p * (1.0 / denom)).astype(BF16)
    for j, g in blocks:
        o_t = _dot(v_cats[j][g * ATT_HEAD_DIM:(g + 1) * ATT_HEAD_DIM], probs[j, g])
        stacked = jnp.concatenate([o_t[:, i * WINDOW:(i + 1) * WINDOW] for i in range(ATT_GROUP)], axis=0)
        ya_ref[j * WINDOW:(j + 1) * WINDOW, g * GROUP_WIDTH:(g + 1) * GROUP_WIDTH] = stacked.T

    for g in range(ATT_KV_HEADS):
        kprev_ref[g] = krep[g][tm - WINDOW:tm].astype(F32)
    vprev_ref[...] = v_t[:, tm - WINDOW:tm]

    ya = ya_ref[...] * _silu(za)

    pb = proj(wmain_ref, OFF_B, OFF_C)
    ub = pb[:, 0:CONV_WIDTH] * _sigmoid(pb[:, CONV_WIDTH:2 * CONV_WIDTH])
    zb = pb[:, 2 * CONV_WIDTH:]
    convbuf_ref[CONV_PAD:CONV_PAD + tm, :] = ub
    base = CONV_PAD - (CONV_K - 1)
    phase_rows = phase_ref.shape[1]
    for ph in range(1, SUBLANES):
        phase_ref[ph - 1] = convbuf_ref[ph:ph + phase_rows, :]
    sconvbuf_ref[SCONV_PAD:SCONV_PAD + tm, :] = proj(wmain_ref, OFF_C, MAIN_COLS)
    for piece in range(3 * D_MODEL // GATE_COLS):
        cols = slice(piece * GATE_COLS, (piece + 1) * GATE_COLS)
        gates_ref[:, cols] = _sigmoid(
            proj(wtail_ref, TAIL_MG + piece * GATE_COLS, TAIL_MG + (piece + 1) * GATE_COLS))
    dw_b = vec(V_DW_B, CONV_WIDTH)
    for r in range(tm // CONV_ROWS):
        acc = jnp.broadcast_to(dw_b, (CONV_ROWS, CONV_WIDTH))
        for k in range(CONV_K):
            ph = (base + k) % SUBLANES
            start = (base + k) - ph + r * CONV_ROWS
            if ph == 0:
                win = convbuf_ref[start:start + CONV_ROWS, :]
            else:
                win = phase_ref[ph - 1, start:start + CONV_ROWS, :]
            acc = acc + jnp.concatenate([dww_ref[k]] * (CONV_ROWS // SUBLANES), axis=0) * win
        cv_ref[r * CONV_ROWS:(r + 1) * CONV_ROWS, :] = acc
    convbuf_ref[0:CONV_PAD, :] = convbuf_ref[tm:tm + CONV_PAD, :]
    cv = cv_ref[...]
    mu = jnp.mean(cv, axis=-1, keepdims=True)
    xc = cv - mu
    var = jnp.mean(xc * xc, axis=-1, keepdims=True)
    ln = xc * lax.rsqrt(var + EPS) * vec(V_LN_G, CONV_WIDTH) + vec(V_LN_B, CONV_WIDTH)
    yb = (_dot(_silu(ln).astype(BF16), pw2w_ref[...]) + vec(V_PW2_B, CONV_WIDTH)) * _silu(zb)
    merged = _dot(ya.astype(BF16), wpa_ref[...]) * gates_ref[:, 0:D_MODEL]
    merged = merged + _dot(yb.astype(BF16), wpb_ref[...]) * gates_ref[:, D_MODEL:2 * D_MODEL]

    sbase = SCONV_PAD - (DN_CONV_K - 1)
    for r in range(tm // CONV_ROWS):
        for cblk in range(3):
            cols = slice(cblk * DN_WIDTH, (cblk + 1) * DN_WIDTH)
            acc = None
            for k in range(DN_CONV_K):
                start = sbase + k + r * CONV_ROWS
                tap = jnp.concatenate([scw_ref[k, :, cols]] * (CONV_ROWS // SUBLANES), axis=0)
                term = tap * sconvbuf_ref[start:start + CONV_ROWS, cols]
                acc = term if acc is None else acc + term
            qkv_ref[r * CONV_ROWS:(r + 1) * CONV_ROWS, cols] = _silu(acc)
    sconvbuf_ref[0:SCONV_PAD, :] = sconvbuf_ref[tm:tm + SCONV_PAD, :]

    ab = _dot(hb, wrest_ref[:, 0:LANES])
    g_col = -jnp.exp(vec(V_A_LOG, LANES)) * _softplus(ab + vec(V_DT_BIAS, LANES))
    beta_col = _sigmoid(ab)

    n_chunks = tm // DN_CHUNK
    ri = lax.broadcasted_iota(jnp.int32, (tm, tm), 0)
    ci = lax.broadcasted_iota(jnp.int32, (tm, tm), 1)
    same_chunk = (ri // DN_CHUNK) == (ci // DN_CHUNK)
    g_hi = g_col.astype(BF16)
    g_mid = (g_col - g_hi.astype(F32)).astype(BF16)
    g_lo = (g_col - g_hi.astype(F32) - g_mid.astype(F32)).astype(BF16)
    tri = jnp.where(same_chunk & (ri >= ci), 1.0, 0.0).astype(BF16)
    ones = jnp.where(same_chunk, 1.0, 0.0).astype(BF16)
    gc_col = _dot(tri, g_hi) + _dot(tri, g_mid) + _dot(tri, g_lo)
    gl_col = _dot(ones, g_hi) + _dot(ones, g_mid) + _dot(ones, g_lo)
    gc_row = gc_col.T

    heads = range(DN_HEADS)
    chunks = range(n_chunks)
    crows = [slice(c * DN_CHUNK, (c + 1) * DN_CHUNK) for c in chunks]
    cat_w = n_chunks * DN_CHUNK
    cat_row = lax.broadcasted_iota(jnp.int32, (DN_CHUNK, cat_w), 0)
    cat_lane = lax.broadcasted_iota(jnp.int32, (DN_CHUNK, cat_w), 1)
    cat_blk = cat_lane // DN_CHUNK
    cat_col = cat_lane - cat_blk * DN_CHUNK
    lower = cat_row >= cat_col
    strict = cat_row > cat_col
    diag_blk = (lax.broadcasted_iota(jnp.int32, (cat_w, cat_w), 0) // DN_CHUNK
                == lax.broadcasted_iota(jnp.int32, (cat_w, cat_w), 1) // DN_CHUNK)
    key_blk = (lax.broadcasted_iota(jnp.int32, (tm, n_chunks * DN_HEAD_DIM), 0) // DN_CHUNK
               == lax.broadcasted_iota(jnp.int32, (tm, n_chunks * DN_HEAD_DIM), 1) // DN_HEAD_DIM)

    def block_diag(m_cat):
        tiled = jnp.concatenate([m_cat] * n_chunks, axis=0)
        return jnp.where(diag_blk, tiled, jnp.zeros_like(tiled))

    def chunk_cat(v):
        return jnp.concatenate([v[crows[c]] for c in chunks], axis=1)

    qe_l, ke_l, xin_l, gl_l, sc_l, decay_l = [], [], [], [], [], []
    for hd in heads:
        qh = qkv_ref[:, hd * DN_HEAD_DIM:(hd + 1) * DN_HEAD_DIM]
        kh = qkv_ref[:, DN_WIDTH + hd * DN_HEAD_DIM:DN_WIDTH + (hd + 1) * DN_HEAD_DIM]
        vh = qkv_ref[:, 2 * DN_WIDTH + hd * DN_HEAD_DIM:2 * DN_WIDTH + (hd + 1) * DN_HEAD_DIM]
        qh = qh * lax.rsqrt(jnp.sum(qh * qh, axis=-1, keepdims=True) + EPS) * (DN_HEAD_DIM ** -0.5)
        kh = kh * lax.rsqrt(jnp.sum(kh * kh, axis=-1, keepdims=True) + EPS)
        beta = beta_col[:, DN_HEADS + hd:DN_HEADS + hd + 1]
        gc = gc_col[:, hd:hd + 1]
        gl = gl_col[:, hd:hd + 1]
        kb = kh * beta
        eg = jnp.exp(gc)
        qe_l.append(qh * eg)
        ke_l.append(kh * jnp.exp(gl - gc))
        xin_l.append(jnp.concatenate([vh * beta, kb * eg], axis=1))
        gl_l.append(gl)
        kq = jnp.concatenate([chunk_cat(kb), chunk_cat(qh)], axis=0).astype(BF16)
        k_tiled = jnp.concatenate([kh.astype(BF16)] * n_chunks, axis=1)
        k_bd = jnp.where(key_blk, k_tiled, jnp.zeros_like(k_tiled))
        sc_l.append(_dot_nt(kq, k_bd))
        gc_i = gc[crows[n_chunks - 1]]
        for c in reversed(range(n_chunks - 1)):
            gc_i = jnp.where(cat_blk == c, gc[crows[c]], gc_i)
        decay_l.append(jnp.exp(jnp.where(lower, gc_i - gc_row[hd:hd + 1, :], NEG_INF)))

    pw_l = [jnp.where(strict, sc_l[hd][0:DN_CHUNK] * decay_l[hd], 0.0) for hd in heads]
    intra_l = [(sc_l[hd][DN_CHUNK:] * decay_l[hd]).astype(BF16) for hd in heads]
    q_l = [-pw_l[hd] for hd in heads]
    n_steps = int(np.log2(DN_CHUNK))
    for step in range(n_steps):
        bd_l = [block_diag(pw_l[hd].astype(BF16)) for hd in heads]
        if step == 0:
            pw_l = [_dot(pw_l[hd].astype(BF16), bd_l[hd]) for hd in heads]
            continue
        last = step == n_steps - 1
        lhs_l = [q_l[hd].astype(BF16) if last else
                 jnp.concatenate([pw_l[hd], q_l[hd]], axis=0).astype(BF16) for hd in heads]
        r_l = [_dot(lhs_l[hd], bd_l[hd]) for hd in heads]
        q_l = [q_l[hd] + pw_l[hd] + r_l[hd][-DN_CHUNK:] for hd in heads]
        if not last:
            pw_l = [r_l[hd][0:DN_CHUNK] for hd in heads]
    uw_l = [xin_l[hd] + _dot(block_diag(q_l[hd].astype(BF16)), xin_l[hd].astype(BF16)) for hd in heads]

    state = [state_ref[hd] for hd in heads]
    o_inter = [[] for _ in heads]
    v_news = [[] for _ in heads]
    for c in chunks:
        res = []
        for hd in heads:
            wq = jnp.concatenate([uw_l[hd][crows[c], DN_HEAD_DIM:], qe_l[hd][crows[c]]], axis=0).astype(BF16)
            res.append(_dot(wq, state[hd].astype(BF16)))
        for hd in heads:
            v_new = uw_l[hd][crows[c], 0:DN_HEAD_DIM] - res[hd][0:DN_CHUNK]
            v_news[hd].append(v_new)
            o_inter[hd].append(res[hd][DN_CHUNK:])
            g_last = gl_l[hd][c * DN_CHUNK:c * DN_CHUNK + 1, :]
            state[hd] = state[hd] * jnp.exp(g_last) + _dot(ke_l[hd][crows[c]].T.astype(BF16),
                                                           v_new.astype(BF16))
    dn_gain = vec(V_DN_GAIN, DN_HEAD_DIM)
    for hd in heads:
        state_ref[hd] = state[hd]
        v_new_all = jnp.concatenate(v_news[hd], axis=0).astype(BF16)
        od = jnp.concatenate(o_inter[hd], axis=0) + _dot(block_diag(intra_l[hd]), v_new_all)
        od = od * lax.rsqrt(jnp.mean(od * od, axis=-1, keepdims=True) + EPS) * dn_gain
        yc_ref[:, hd * DN_HEAD_DIM:(hd + 1) * DN_HEAD_DIM] = od

    yc = yc_ref[...] * _silu(proj(wtail_ref, TAIL_ZC, TAIL_MG))
    merged = merged + _dot(yc.astype(BF16), wpc_ref[...]) * gates_ref[:, 2 * D_MODEL:3 * D_MODEL]

    o_ref[...] = x + gate * _dot(merged.astype(BF16), wout_ref[...])


def _alibi_bias():
    qi = np.arange(WINDOW)[:, None]
    kj = np.arange(2 * WINDOW)[None, :]
    dist = qi + WINDOW - kj
    valid = (dist >= 0) & (dist < WINDOW)
    slopes = np.exp2(-8.0 * np.arange(1, ATT_HEADS + 1, dtype=np.float32) / ATT_HEADS)
    bias = np.where(valid[None], -LOG2_E * slopes[:, None, None] * dist[None].astype(np.float32), NEG_INF)
    bias = bias.reshape(ATT_KV_HEADS, ATT_GROUP * WINDOW, 2 * WINDOW)
    return jnp.asarray(np.swapaxes(bias, 1, 2), dtype=F32)


def _layer_call(layer, x, mod, p):
    bsz, seq, d = x.shape
    tm = TILE
    assert seq % tm == 0 and d == D_MODEL and bsz <= SUBLANES

    def layer_spec(a):
        nd = a.ndim - 1
        return pl.BlockSpec((None,) + a.shape[1:], lambda b, t: (layer,) + (0,) * nd,
                            pipeline_mode=pl.Buffered(1))

    def const_spec(a):
        nd = a.ndim
        return pl.BlockSpec(a.shape, lambda b, t: (0,) * nd, pipeline_mode=pl.Buffered(1))

    def w_in_spec(col_block):
        return pl.BlockSpec((None, d, MAIN_COLS), lambda b, t: (layer, 0, col_block),
                            pipeline_mode=pl.Buffered(1))

    tile_spec = pl.BlockSpec((None, tm, d), lambda b, t: (b, t, 0))
    after_bias = [p["dw_w"], p["pw2_w"], p["sconv_w"], p["w_proj_a"], p["w_proj_b"], p["w_proj_c"],
                  p["w_out"]]
    return pl.pallas_call(
        _layer_body,
        grid=(bsz, seq // tm),
        in_specs=([tile_spec, layer_spec(mod), layer_spec(p["vecs"]), w_in_spec(0), w_in_spec(1),
                   layer_spec(p["sinks"]), const_spec(p["bias"])]
                  + [layer_spec(a) for a in after_bias]),
        out_specs=tile_spec,
        out_shape=jax.ShapeDtypeStruct(x.shape, x.dtype),
        scratch_shapes=[
            pltpu.VMEM((ATT_KV_HEADS, WINDOW, GROUP_WIDTH), F32),
            pltpu.VMEM((ATT_KV_WIDTH, WINDOW), BF16),
            pltpu.VMEM((CONV_PAD + tm, CONV_WIDTH), F32),
            pltpu.VMEM((SCONV_PAD + tm, 3 * DN_WIDTH), F32),
            pltpu.VMEM((DN_HEADS, DN_HEAD_DIM, DN_HEAD_DIM), F32),
            pltpu.VMEM((tm, ATT_WIDTH), F32),
            pltpu.VMEM((SUBLANES - 1, tm + CONV_PAD - SUBLANES, CONV_WIDTH), F32),
            pltpu.VMEM((tm, CONV_WIDTH), F32),
            pltpu.VMEM((tm, 3 * DN_WIDTH), F32),
            pltpu.VMEM((tm, DN_WIDTH), F32),
            pltpu.VMEM((tm, 3 * D_MODEL), F32),
            pltpu.VMEM((d, TAIL_COLS), BF16),
        ],
        compiler_params=pltpu.CompilerParams(
            dimension_semantics=("arbitrary", "arbitrary"),
            vmem_limit_bytes=VMEM_LIMIT),
        name="hybrid_layer",
    )(x, mod, p["vecs"], p["w_in"], p["w_in"], p["sinks"], p["bias"], *after_bias)


def _prep_params(norm_g, w_in, q_norm_g, k_norm_g, sinks, dw_w, dw_b, ln_g, ln_b, pw2_w, pw2_b,
                 sconv_w, a_log, dt_bias, dn_norm_g, w_proj_a, w_proj_b, w_proj_c, w_out):
    depth, d, _ = w_in.shape

    def row(v):
        return jnp.pad(v, ((0, 0), (0, D_MODEL - v.shape[1])))[:, None, :]

    vec_rows = [None] * N_VECS
    vec_rows[V_NORM_G] = row(norm_g)
    vec_rows[V_Q_GAIN] = row(jnp.tile(q_norm_g, (1, ATT_HEADS)) * (ATT_HEAD_DIM ** -0.5 * LOG2_E))
    vec_rows[V_K_GAIN] = row(jnp.tile(k_norm_g, (1, ATT_KV_HEADS)))
    vec_rows[V_DW_B] = row(dw_b)
    vec_rows[V_LN_G] = row(ln_g)
    vec_rows[V_LN_B] = row(ln_b)
    vec_rows[V_PW2_B] = row(pw2_b)
    vec_rows[V_A_LOG] = row(a_log)
    vec_rows[V_DT_BIAS] = row(dt_bias)
    vec_rows[V_DN_GAIN] = row(dn_norm_g)
    return dict(
        vecs=jnp.stack(vec_rows, axis=1),
        w_in=_cast_call(w_in),
        sinks=jnp.repeat(sinks * LOG2_E, WINDOW, axis=1).reshape(depth, ATT_KV_HEADS, 1, ATT_GROUP * WINDOW),
        bias=_alibi_bias(),
        dw_w=jnp.broadcast_to(dw_w[:, :, None, :], (depth, CONV_K, SUBLANES, CONV_WIDTH)),
        pw2_w=pw2_w.astype(BF16),
        sconv_w=jnp.broadcast_to(sconv_w[:, :, None, :], (depth, DN_CONV_K, SUBLANES, 3 * DN_WIDTH)),
        w_proj_a=w_proj_a.astype(BF16),
        w_proj_b=w_proj_b.astype(BF16),
        w_proj_c=w_proj_c.astype(BF16),
        w_out=w_out.astype(BF16),
    )


@jax.jit
def _forward(x, c, w_ada, b_ada, norm_g, w_in, q_norm_g, k_norm_g, sinks, dw_w, dw_b, ln_g, ln_b,
             pw2_w, pw2_b, sconv_w, a_log, dt_bias, dn_norm_g, w_proj_a, w_proj_b, w_proj_c, w_out):
    bsz, _, d = x.shape
    depth = w_ada.shape[0]
    c_pad = jnp.zeros((SUBLANES, d), F32).at[:bsz].set(c)
    mod = _ada_call(c_pad, w_ada, b_ada)
    p = _prep_params(norm_g, w_in, q_norm_g, k_norm_g, sinks, dw_w, dw_b, ln_g, ln_b, pw2_w, pw2_b,
                     sconv_w, a_log, dt_bias, dn_norm_g, w_proj_a, w_proj_b, w_proj_c, w_out)
    for layer in range(depth):
        x = _layer_call(layer, x, mod, p)
    return x


def kernel(x, c, w_ada, b_ada, norm_g, w_in, q_norm_g, k_norm_g, sinks, dw_w, dw_b, ln_g, ln_b, pw2_w, pw2_b, sconv_w, a_log, dt_bias, dn_norm_g, w_proj_a, w_proj_b, w_proj_c, w_out):
    return _forward(x, c, w_ada, b_ada, norm_g, w_in, q_norm_g, k_norm_g, sinks, dw_w, dw_b, ln_g, ln_b,
                    pw2_w, pw2_b, sconv_w, a_log, dt_bias, dn_norm_g, w_proj_a, w_proj_b, w_proj_c, w_out)
```
